```python
import math
import jax, jax.numpy as jnp
from jax import lax
import numpy as np

D_MODEL = 1024
BATCH = 8
SEQ = 2048
DEPTH = 2
DEC_BATCH = 128
DEC_SEQ = 8
PAST_LEN = 16384
PAGE_SIZE = 128

N_MIXERS = 2
N_A_LAYERS = (DEPTH + 1) // 2
N_B_LAYERS = DEPTH // 2
CHUNK = 128
D_GMLP = 3 * D_MODEL
G_A = 8
GW_A = D_GMLP // G_A
D_RNN = D_MODEL
H_B = 8
BW_B = D_RNN // H_B
CONV_W = 4
LRU_C = 8.0
D_FF = ((8 * D_MODEL + 3 * 256 - 1) // (3 * 256)) * 256
EPS = 1e-6

kernel_name = "hybrid_gmlp_rglru_decode_step"


def rms_norm(x, g):
    xf = x.astype(jnp.float32)
    y = xf * lax.rsqrt(jnp.mean(xf * xf, axis=-1, keepdims=True) + EPS)
    return (y * g.astype(jnp.float32)).astype(x.dtype)


def layer_norm(x, g, b):
    xf = x.astype(jnp.float32)
    mu = jnp.mean(xf, axis=-1, keepdims=True)
    xc = xf - mu
    y = xc * lax.rsqrt(jnp.mean(xc * xc, axis=-1, keepdims=True) + EPS)
    return (y * g.astype(jnp.float32) + b.astype(jnp.float32)).astype(x.dtype)


def gmlp_mixer(h, w_in, b_in, ln_g, ln_b, w_s, b_s, w_out):
    B, T, _ = h.shape
    z = jax.nn.gelu(h @ w_in + b_in)
    u, v = jnp.split(z, 2, axis=-1)
    v = layer_norm(v, ln_g, ln_b)
    n_chunks = -(-T // CHUNK)
    pad = n_chunks * CHUNK - T
    vp = jnp.pad(v, ((0, 0), (0, pad), (0, 0))).reshape(B, n_chunks, CHUNK, G_A, GW_A)
    mask = jnp.tril(jnp.ones((CHUNK, CHUNK), dtype=bool))
    ws = jnp.where(mask[None], w_s, jnp.zeros_like(w_s))
    mixed = jnp.einsum('gts,bcsge->bctge', ws, vp) + jnp.transpose(b_s)[None, None, :, :, None]
    mixed = mixed.reshape(B, n_chunks * CHUNK, D_GMLP)[:, :T]
    return (u * mixed) @ w_out, v


def _lin_comb(left, right):
    a_l, b_l = left
    a_r, b_r = right
    return a_l * a_r, a_r * b_l + b_r


def rglru_mixer(h, conv_hist, h0, w_in, conv_w, conv_b, w_a, b_a, w_x, b_x, lam, w_out):
    B, T, _ = h.shape
    y_br, x_br = jnp.split(h @ w_in, 2, axis=-1)
    gate = jax.nn.gelu(y_br)
    xpad = jnp.concatenate([conv_hist.astype(x_br.dtype), x_br], axis=1)
    xc = conv_b + sum(conv_w[k] * xpad[:, k:k + T] for k in range(CONV_W))
    new_hist = xpad[:, T:]
    xb = xc.reshape(B, T, H_B, BW_B)
    r = jax.nn.sigmoid(jnp.einsum('bthi,hij->bthj', xb, w_a) + b_a).reshape(B, T, D_RNN)
    i = jax.nn.sigmoid(jnp.einsum('bthi,hij->bthj', xb, w_x) + b_x).reshape(B, T, D_RNN)
    log_a = -LRU_C * r.astype(jnp.float32) * jax.nn.softplus(-lam.astype(jnp.float32))
    a = jnp.exp(log_a)
    mult = jnp.sqrt(-jnp.expm1(2.0 * log_a))
    b = mult * (i * xc).astype(jnp.float32)
    b = b.at[:, 0].add(a[:, 0] * h0.astype(jnp.float32))
    _, hs = lax.associative_scan(_lin_comb, (a, b), axis=1)
    out = (hs.astype(h.dtype) * gate) @ w_out
    return out, new_hist, hs[:, -1].astype(h.dtype)


def swiglu(h, w_in, w_out):
    g, u = jnp.split(h @ w_in, 2, axis=-1)
    return (jax.nn.silu(g) * u) @ w_out


def trunk(x, conv_hist, h0, norm_gains,
          gm_w_in, gm_b_in, gm_ln_g, gm_ln_b, gm_w_s, gm_b_s, gm_w_out,
          rg_w_in, rg_conv_w, rg_conv_b, rg_w_a, rg_b_a, rg_w_x, rg_b_x, rg_lambda, rg_w_out,
          ffn_w_in, ffn_w_out):
    new_conv, new_h, v_rows = [], [], []
    for li in range(DEPTH):
        g = norm_gains[li]
        hn = rms_norm(x, g[0])
        j = li // N_MIXERS
        if li % N_MIXERS == 0:
            m, v = gmlp_mixer(hn, gm_w_in[j], gm_b_in[j], gm_ln_g[j], gm_ln_b[j],
                              gm_w_s[j], gm_b_s[j], gm_w_out[j])
            v_rows.append(v)
        else:
            m, hist, hl = rglru_mixer(hn, conv_hist[j], h0[j], rg_w_in[j], rg_conv_w[j], rg_conv_b[j],
                                      rg_w_a[j], rg_b_a[j], rg_w_x[j], rg_b_x[j], rg_lambda[j], rg_w_out[j])
            new_conv.append(hist)
            new_h.append(hl)
        x = x + rms_norm(m, g[1])
        f = swiglu(rms_norm(x, g[2]), ffn_w_in[li], ffn_w_out[li])
        x = x + rms_norm(f, g[3])
    return x, jnp.stack(new_conv), jnp.stack(new_h), jnp.stack(v_rows)


def setup_inputs(seed: int = 0) -> dict:
    key = jax.random.key(seed)
    ks = jax.random.split(key, 24)
    f32 = jnp.float32
    nrm = lambda k, shape, scale: jax.random.normal(k, shape, f32) * scale
    u = jax.random.uniform(ks[20], (N_B_LAYERS, D_RNN), f32, 0.9, 0.999)
    s = u ** (1.0 / LRU_C)
    return {
        "x_prompt": nrm(ks[0], (BATCH, SEQ, D_MODEL), 1.0),
        "x_sample": nrm(ks[1], (DEC_BATCH, DEC_SEQ, D_MODEL), 1.0),
        "state_conv": nrm(ks[2], (N_B_LAYERS, DEC_BATCH, CONV_W - 1, D_RNN), 1.0),
        "state_h": nrm(ks[3], (N_B_LAYERS, DEC_BATCH, D_RNN), 1.0),
        "norm_gains": 1.0 + nrm(ks[4], (DEPTH, 4, D_MODEL), 0.02),
        "gm_w_in": nrm(ks[5], (N_A_LAYERS, D_MODEL, 2 * D_GMLP), D_MODEL ** -0.5),
        "gm_b_in": nrm(ks[6], (N_A_LAYERS, 2 * D_GMLP), 0.02),
        "gm_ln_g": 1.0 + nrm(ks[7], (N_A_LAYERS, D_GMLP), 0.02),
        "gm_ln_b": nrm(ks[8], (N_A_LAYERS, D_GMLP), 0.02),
        "gm_w_s": nrm(ks[9], (N_A_LAYERS, G_A, CHUNK, CHUNK), CHUNK ** -0.5),
        "gm_b_s": 1.0 + nrm(ks[10], (N_A_LAYERS, G_A, CHUNK), 0.02),
        "gm_w_out": nrm(ks[11], (N_A_LAYERS, D_GMLP, D_MODEL), D_GMLP ** -0.5),
        "rg_w_in": nrm(ks[12], (N_B_LAYERS, D_MODEL, 2 * D_RNN), D_MODEL ** -0.5),
        "rg_conv_w": nrm(ks[13], (N_B_LAYERS, CONV_W, D_RNN), CONV_W ** -0.5),
        "rg_conv_b": nrm(ks[14], (N_B_LAYERS, D_RNN), 0.02),
        "rg_w_a": nrm(ks[15], (N_B_LAYERS, H_B, BW_B, BW_B), BW_B ** -0.5),
        "rg_b_a": nrm(ks[16], (N_B_LAYERS, H_B, BW_B), 0.02),
        "rg_w_x": nrm(ks[17], (N_B_LAYERS, H_B, BW_B, BW_B), BW_B ** -0.5),
        "rg_b_x": nrm(ks[18], (N_B_LAYERS, H_B, BW_B), 0.02),
        "rg_lambda": jnp.log(s) - jnp.log1p(-s),
        "rg_w_out": nrm(ks[19], (N_B_LAYERS, D_RNN, D_MODEL), D_RNN ** -0.5),
        "ffn_w_in": nrm(ks[21], (DEPTH, D_MODEL, 2 * D_FF), D_MODEL ** -0.5),
        "ffn_w_out": nrm(ks[22], (DEPTH, D_FF, D_MODEL), D_FF ** -0.5),
    }


def reference(x_prompt, x_sample, state_conv, state_h, norm_gains,
              gm_w_in, gm_b_in, gm_ln_g, gm_ln_b, gm_w_s, gm_b_s, gm_w_out,
              rg_w_in, rg_conv_w, rg_conv_b, rg_w_a, rg_b_a, rg_w_x, rg_b_x, rg_lambda, rg_w_out,
              ffn_w_in, ffn_w_out):
    weights = (norm_gains, gm_w_in, gm_b_in, gm_ln_g, gm_ln_b, gm_w_s, gm_b_s, gm_w_out,
               rg_w_in, rg_conv_w, rg_conv_b, rg_w_a, rg_b_a, rg_w_x, rg_b_x, rg_lambda, rg_w_out,
               ffn_w_in, ffn_w_out)
    zero_conv = jnp.zeros((N_B_LAYERS, x_prompt.shape[0], CONV_W - 1, D_RNN), x_prompt.dtype)
    zero_h = jnp.zeros((N_B_LAYERS, x_prompt.shape[0], D_RNN), x_prompt.dtype)
    y_prompt, p_conv, p_h, _ = trunk(x_prompt, zero_conv, zero_h, *weights)
    y_sample, s_conv, s_h, s_v = trunk(x_sample, state_conv, state_h, *weights)
    return (y_prompt, y_sample, p_conv, p_h, s_conv, s_h, s_v)
```

```python
import functools
import math

import jax
import jax.numpy as jnp
from jax import lax
from jax.experimental import pallas as pl
from jax.experimental.pallas import tpu as pltpu

D_MODEL = 1024
CHUNK = 128
D_GMLP = 3 * D_MODEL
G_A = 8
GW_A = D_GMLP // G_A
D_RNN = D_MODEL
H_B = 8
BW_B = D_RNN // H_B
CONV_W = 4
LRU_C = 8.0
EPS = 1e-6

LANES = 128
SUBLANES = 8
VMEM_LIMIT_BYTES = 56 * 2 ** 20

BF16 = jnp.bfloat16
F32 = jnp.float32

_GELU_C = math.sqrt(2.0 / math.pi)


def _gelu(x):
    return 0.5 * x * (1.0 + jnp.tanh(_GELU_C * (x + 0.044715 * (x * x * x))))


def _sigmoid(x):
    return 1.0 / (1.0 + jnp.exp(-x))


def _rms(x, g):
    ms = jnp.sum(x * x, axis=-1, keepdims=True) * (1.0 / x.shape[-1])
    return x * lax.rsqrt(ms + EPS) * g


def _dot(a, b):
    return jnp.dot(a, b, preferred_element_type=F32)


def _const_spec(shape):
    return pl.BlockSpec(shape, lambda i: (0,) * len(shape), pipeline_mode=pl.Buffered(1))


def _params():
    return pltpu.CompilerParams(dimension_semantics=("arbitrary",),
                                vmem_limit_bytes=VMEM_LIMIT_BYTES)


_GM_NC = 768


def _gmlp_kernel(x_ref, g_ref, w_in_ref, b_in_ref, ln_g_ref, ln_b_ref, ws_ref, bs_ref, w_out_ref,
                 *rest, tm, period, emit_v):
    if emit_v:
        y_ref, v_ref, v_scr, vn_scr, p_scr = rest
    else:
        y_ref, v_scr, vn_scr, p_scr = rest
    x = x_ref[...]
    hn = _rms(x, g_ref[0:1, :]).astype(BF16)

    n_half = D_GMLP // _GM_NC
    for j in range(n_half):
        c0 = D_GMLP + j * _GM_NC
        z = _dot(hn, w_in_ref[:, c0:c0 + _GM_NC]) + b_in_ref[:, c0:c0 + _GM_NC]
        v_scr[:, j * _GM_NC:(j + 1) * _GM_NC] = _gelu(z)

    v = v_scr[...]
    mu = jnp.sum(v, axis=-1, keepdims=True) * (1.0 / D_GMLP)
    vc = v - mu
    var = jnp.sum(vc * vc, axis=-1, keepdims=True) * (1.0 / D_GMLP)
    vn = vc * lax.rsqrt(var + EPS) * ln_g_ref[...] + ln_b_ref[...]
    if emit_v:
        v_ref[...] = vn
    vn_scr[...] = vn.astype(BF16)

    row = lax.broadcasted_iota(jnp.int32, (CHUNK, CHUNK), 0)
    col = lax.broadcasted_iota(jnp.int32, (CHUNK, CHUNK), 1)
    keep = (col <= row) & (col >= row - (row % period))

    for j in range(n_half):
        c0 = j * _GM_NC
        u = _gelu(_dot(hn, w_in_ref[:, c0:c0 + _GM_NC]) + b_in_ref[:, c0:c0 + _GM_NC])
        for gg in range(_GM_NC // GW_A):
            g = j * (_GM_NC // GW_A) + gg
            wsm = jnp.where(keep, ws_ref[g], 0.0).astype(BF16)
            bias = bs_ref[:, g:g + 1]
            for c in range(tm // CHUNK):
                r0 = c * CHUNK
                mixed = _dot(wsm, vn_scr[r0:r0 + CHUNK, g * GW_A:(g + 1) * GW_A]) + bias
                p_scr[r0:r0 + CHUNK, g * GW_A:(g + 1) * GW_A] = (
                    u[r0:r0 + CHUNK, gg * GW_A:(gg + 1) * GW_A] * mixed).astype(BF16)

    m = _dot(p_scr[...], w_out_ref[...])
    y_ref[...] = x + _rms(m, g_ref[1:2, :])


def _gmlp_layer(x, gains, w_in, b_in, ln_g, ln_b, ws, bs_t, w_out, *, tm, period, emit_v):
    rows = x.shape[0]
    assert rows % tm == 0 and tm % CHUNK == 0 and CHUNK % period == 0
    out_shape = [jax.ShapeDtypeStruct((rows, D_MODEL), F32)]
    out_specs = [pl.BlockSpec((tm, D_MODEL), lambda i: (i, 0))]
    if emit_v:
        out_shape.append(jax.ShapeDtypeStruct((rows, D_GMLP), F32))
        out_specs.append(pl.BlockSpec((tm, D_GMLP), lambda i: (i, 0)))
    res = pl.pallas_call(
        functools.partial(_gmlp_kernel, tm=tm, period=period, emit_v=emit_v),
        grid=(rows // tm,),
        in_specs=[
            pl.BlockSpec((tm, D_MODEL), lambda i: (i, 0)),
            _const_spec(gains.shape), _const_spec(w_in.shape), _const_spec(b_in.shape),
            _const_spec(ln_g.shape), _const_spec(ln_b.shape), _const_spec(ws.shape),
            _const_spec(bs_t.shape), _const_spec(w_out.shape),
        ],
        out_specs=out_specs,
        out_shape=out_shape,
        scratch_shapes=[pltpu.VMEM((tm, D_GMLP), F32), pltpu.VMEM((tm, D_GMLP), BF16),
                        pltpu.VMEM((tm, D_GMLP), BF16)],
        compiler_params=_params(),
        name="gmlp_mixer",
    )(x, gains, w_in, b_in, ln_g, ln_b, ws, bs_t, w_out)
    return res


_FF_NC = 256


def _ffn_kernel(x_ref, g_ref, w_in_ref, w_out_ref, y_ref, p_scr, *, d_ff):
    x = x_ref[...]
    hn = _rms(x, g_ref[2:3, :]).astype(BF16)
    for j in range(d_ff // _FF_NC):
        c0 = j * _FF_NC
        g = _dot(hn, w_in_ref[:, c0:c0 + _FF_NC])
        u = _dot(hn, w_in_ref[:, d_ff + c0:d_ff + c0 + _FF_NC])
        p_scr[:, c0:c0 + _FF_NC] = (g * _sigmoid(g) * u).astype(BF16)
    f = _dot(p_scr[...], w_out_ref[...])
    y_ref[...] = x + _rms(f, g_ref[3:4, :])


def _ffn_layer(x, gains, w_in, w_out, *, tm):
    rows = x.shape[0]
    d_ff = w_out.shape[0]
    assert rows % tm == 0 and d_ff % _FF_NC == 0
    return pl.pallas_call(
        functools.partial(_ffn_kernel, d_ff=d_ff),
        grid=(rows // tm,),
        in_specs=[pl.BlockSpec((tm, D_MODEL), lambda i: (i, 0)),
                  _const_spec(gains.shape), _const_spec(w_in.shape), _const_spec(w_out.shape)],
        out_specs=pl.BlockSpec((tm, D_MODEL), lambda i: (i, 0)),
        out_shape=jax.ShapeDtypeStruct((rows, D_MODEL), F32),
        scratch_shapes=[pltpu.VMEM((tm, d_ff), BF16)],
        compiler_params=_params(),
        name="swiglu",
    )(x, gains, w_in, w_out)


_HIST = CONV_W - 1


def _scan_pitch(tt):
    return tt if (tt // SUBLANES) % 2 == 1 else tt + SUBLANES


def _rglru_kernel(x_ref, hist_ref, h0_ref, g_ref, w_in_ref, cw_ref, cb_ref, wax_ref, bax_ref,
                  lam_ref, w_out_ref, y_ref, hist_out_ref, h_out_ref,
                  xp_scr, gate_scr, a_scr, b_scr, h_scr, *, nb, tt):
    m_rows = nb * tt
    pitch = _scan_pitch(tt)
    lead = SUBLANES

    @pl.when(pl.program_id(0) == 0)
    def _():
        xp_scr[:, lead - _HIST:lead, :] = hist_ref[...]
        h_scr[...] = h0_ref[...]

    x = x_ref[...].reshape(m_rows, D_MODEL)
    hn = _rms(x, g_ref[0:1, :]).astype(BF16)
    gate_scr[...] = _gelu(_dot(hn, w_in_ref[:, :D_RNN]))
    xb = _dot(hn, w_in_ref[:, D_RNN:])
    xp_scr[:, lead:lead + tt, :] = xb.reshape(nb, tt, D_RNN)

    xc = cb_ref[...].reshape(1, 1, D_RNN)
    for k in range(CONV_W):
        s0 = lead - _HIST + k
        xc = xc + cw_ref[k:k + 1, :].reshape(1, 1, D_RNN) * xp_scr[:, s0:s0 + tt, :]
    new_hist = xp_scr[:, lead + tt - _HIST:lead + tt, :]
    hist_out_ref[...] = new_hist
    xp_scr[:, lead - _HIST:lead, :] = new_hist
    xc = xc.reshape(m_rows, D_RNN)
    xcb = xc.astype(BF16)

    lam = lam_ref[...]
    neg_c_sp = -LRU_C * (jnp.maximum(-lam, 0.0) + jnp.log1p(jnp.exp(-jnp.abs(lam))))

    for h in range(H_B):
        hs = slice(h * BW_B, (h + 1) * BW_B)
        ri = _dot(xcb[:, hs], wax_ref[h]) + bax_ref[h:h + 1, :]
        r = _sigmoid(ri[:, :BW_B])
        ig = _sigmoid(ri[:, BW_B:])
        log_a = neg_c_sp[:, hs] * r
        a = jnp.exp(log_a)
        mult = jnp.sqrt(-jnp.tanh(log_a) * (1.0 + a * a))
        bb = mult * (ig * xc[:, hs])
        for b in range(nb):
            a_scr[h, b * pitch:b * pitch + tt, :] = a[b * tt:(b + 1) * tt, :]
            b_scr[h, b * pitch:b * pitch + tt, :] = bb[b * tt:(b + 1) * tt, :]

    for s in range(D_RNN // LANES):
        hcur = h_scr[:, s * LANES:(s + 1) * LANES]
        for t in range(tt):
            idx = pl.ds(t, nb, stride=pitch)
            hcur = a_scr[s, idx, :] * hcur + b_scr[s, idx, :]
            b_scr[s, idx, :] = hcur
        h_scr[:, s * LANES:(s + 1) * LANES] = hcur
    h_out_ref[...] = h_scr[...]

    hs_all = jnp.concatenate(
        [jnp.concatenate([b_scr[s, b * pitch:b * pitch + tt, :] for b in range(nb)], axis=0)
         for s in range(D_RNN // LANES)], axis=1)
    og = (hs_all * gate_scr[...]).astype(BF16)
    m = _dot(og, w_out_ref[...])
    y_ref[...] = (x + _rms(m, g_ref[1:2, :])).reshape(nb, tt, D_MODEL)


def _rglru_layer(x, hist, h0, gains, w_in, cw, cb, wax, bax, lam, w_out, *, tt):
    nb, t_len, _ = x.shape
    assert t_len % tt == 0 and tt % SUBLANES == 0
    pitch = _scan_pitch(tt)
    m_rows = nb * tt
    return pl.pallas_call(
        functools.partial(_rglru_kernel, nb=nb, tt=tt),
        grid=(t_len // tt,),
        in_specs=[
            pl.BlockSpec((nb, tt, D_MODEL), lambda i: (0, i, 0)),
            _const_spec(hist.shape), _const_spec(h0.shape), _const_spec(gains.shape),
            _const_spec(w_in.shape), _const_spec(cw.shape), _const_spec(cb.shape),
            _const_spec(wax.shape), _const_spec(bax.shape), _const_spec(lam.shape),
            _const_spec(w_out.shape),
        ],
        out_specs=[
            pl.BlockSpec((nb, tt, D_MODEL), lambda i: (0, i, 0)),
            pl.BlockSpec((nb, _HIST, D_RNN), lambda i: (0, 0, 0)),
            pl.BlockSpec((nb, D_RNN), lambda i: (0, 0)),
        ],
        out_shape=[
            jax.ShapeDtypeStruct((nb, t_len, D_MODEL), F32),
            jax.ShapeDtypeStruct((nb, _HIST, D_RNN), F32),
            jax.ShapeDtypeStruct((nb, D_RNN), F32),
        ],
        scratch_shapes=[
            pltpu.VMEM((nb, tt + SUBLANES, D_RNN), F32),
            pltpu.VMEM((m_rows, D_RNN), F32),
            pltpu.VMEM((D_RNN // LANES, nb * pitch, LANES), F32),
            pltpu.VMEM((D_RNN // LANES, nb * pitch, LANES), F32),
            pltpu.VMEM((nb, D_RNN), F32),
        ],
        compiler_params=_params(),
        name="rglru_mixer",
    )(x, hist, h0, gains, w_in, cw, cb, wax, bax, lam, w_out)


def kernel(x_prompt, x_sample, state_conv, state_h, norm_gains, gm_w_in, gm_b_in, gm_ln_g, gm_ln_b,
           gm_w_s, gm_b_s, gm_w_out, rg_w_in, rg_conv_w, rg_conv_b, rg_w_a, rg_b_a, rg_w_x, rg_b_x,
           rg_lambda, rg_w_out, ffn_w_in, ffn_w_out):
    n_p, t_p, _ = x_prompt.shape
    n_s, t_s, _ = x_sample.shape
    assert CHUNK % t_s == 0 and t_p % CHUNK == 0

    gm_w_in_b = gm_w_in[0].astype(BF16)
    gm_w_out_b = gm_w_out[0].astype(BF16)
    gm_b_in2 = gm_b_in[0].reshape(1, -1)
    gm_ln_g2 = gm_ln_g[0].reshape(1, -1)
    gm_ln_b2 = gm_ln_b[0].reshape(1, -1)
    ws_p = gm_w_s[0]
    bs_p = gm_b_s[0].T
    rep = CHUNK // t_s
    ws_s = jnp.tile(gm_w_s[0][:, :t_s, :t_s], (1, rep, rep))
    bs_s = jnp.tile(gm_b_s[0][:, :t_s], (1, rep)).T
    rg_w_in_b = rg_w_in[0].astype(BF16)
    rg_w_out_b = rg_w_out[0].astype(BF16)
    rg_wax = jnp.concatenate([rg_w_a[0], rg_w_x[0]], axis=-1).astype(BF16)
    rg_bax = jnp.concatenate([rg_b_a[0], rg_b_x[0]], axis=-1)
    rg_cb = rg_conv_b[0].reshape(1, -1)
    rg_lam = rg_lambda[0].reshape(1, -1)
    ffn_w_in_b = ffn_w_in.astype(BF16)
    ffn_w_out_b = ffn_w_out.astype(BF16)

    def gmlp(x, ws, bs_t, period, emit_v, tm):
        return _gmlp_layer(x, norm_gains[0], gm_w_in_b, gm_b_in2, gm_ln_g2, gm_ln_b2, ws, bs_t,
                           gm_w_out_b, tm=tm, period=period, emit_v=emit_v)

    def ffn(x, li, tm):
        return _ffn_layer(x, norm_gains[li], ffn_w_in_b[li], ffn_w_out_b[li], tm=tm)

    def rglru(x, hist, h0, tt):
        return _rglru_layer(x, hist, h0, norm_gains[1], rg_w_in_b, rg_conv_w[0], rg_cb, rg_wax,
                            rg_bax, rg_lam, rg_w_out_b, tt=tt)

    xp = x_prompt.reshape(n_p * t_p, D_MODEL)
    (xp,) = gmlp(xp, ws_p, bs_p, CHUNK, False, 256)
    xp = ffn(xp, 0, 512)
    xp, p_conv, p_h = rglru(xp.reshape(n_p, t_p, D_MODEL),
                            jnp.zeros((n_p, _HIST, D_RNN), F32), jnp.zeros((n_p, D_RNN), F32), 64)
    y_prompt = ffn(xp.reshape(n_p * t_p, D_MODEL), 1, 512).reshape(n_p, t_p, D_MODEL)

    xs = x_sample.reshape(n_s * t_s, D_MODEL)
    xs, s_v = gmlp(xs, ws_s, bs_s, t_s, True, 256)
    xs = ffn(xs, 0, 512)
    xs, s_conv, s_h = rglru(xs.reshape(n_s, t_s, D_MODEL), state_conv[0], state_h[0], t_s)
    y_sample = ffn(xs.reshape(n_s * t_s, D_MODEL), 1, 512).reshape(n_s, t_s, D_MODEL)

    return (y_prompt, y_sample, p_conv[None], p_h[None], s_conv[None], s_h[None],
            s_v.reshape(1, n_s, t_s, D_GMLP))
```

```python
import functools
import math

import jax
import jax.numpy as jnp
from jax import lax
from jax.experimental import pallas as pl
from jax.experimental.pallas import tpu as pltpu

D_MODEL = 1024
CHUNK = 128
D_GMLP = 3 * D_MODEL
G_A = 8
GW_A = D_GMLP // G_A
D_RNN = D_MODEL
H_B = 8
BW_B = D_RNN // H_B
CONV_W = 4
LRU_C = 8.0
EPS = 1e-6

LANES = 128
SUBLANES = 8
MXU_N = 256
VMEM_LIMIT_BYTES = 56 * 2 ** 20

BF16 = jnp.bfloat16
F32 = jnp.float32

_LOG2E = math.log2(math.e)
_GELU_C = math.sqrt(2.0 / math.pi)
_GELU_K1 = -2.0 * _GELU_C * _LOG2E
_GELU_K3 = -2.0 * _GELU_C * 0.044715 * _LOG2E


def _gelu(x):
    return x * (1.0 / (1.0 + jnp.exp2(x * (_GELU_K1 + _GELU_K3 * (x * x)))))


def _sigmoid(x):
    return 1.0 / (1.0 + jnp.exp2(x * (-_LOG2E)))


def _rms(x, g):
    ms = jnp.sum(x * x, axis=-1, keepdims=True) * (1.0 / x.shape[-1])
    return x * lax.rsqrt(ms + EPS) * g


def _dot(a, b):
    return jnp.dot(a, b, preferred_element_type=F32)


def _dot_cols(a, w_ref, c0, n):
    parts = [_dot(a, w_ref[:, c:c + min(MXU_N, c0 + n - c)]) for c in range(c0, c0 + n, MXU_N)]
    return parts[0] if len(parts) == 1 else jnp.concatenate(parts, axis=1)


def _const_spec(shape):
    return pl.BlockSpec(shape, lambda i: (0,) * len(shape), pipeline_mode=pl.Buffered(1))


def _params():
    return pltpu.CompilerParams(dimension_semantics=("arbitrary",),
                                vmem_limit_bytes=VMEM_LIMIT_BYTES)


_GM_NC = 768


def _gmlp_kernel(x_ref, g_ref, w_in_ref, b_in_ref, ln_g_ref, ln_b_ref, ws_ref, bs_ref, w_out_ref,
                 *rest, tm, period, emit_v):
    if emit_v:
        y_ref, v_ref, v_scr, vn_scr, p_scr = rest
    else:
        y_ref, v_scr, vn_scr, p_scr = rest
    x = x_ref[...]
    hn = _rms(x, g_ref[0:1, :]).astype(BF16)

    n_half = D_GMLP // _GM_NC
    for j in range(D_GMLP // MXU_N):
        c0 = D_GMLP + j * MXU_N
        z = _dot(hn, w_in_ref[:, c0:c0 + MXU_N]) + b_in_ref[:, c0:c0 + MXU_N]
        v_scr[:, j * MXU_N:(j + 1) * MXU_N] = _gelu(z)

    v = v_scr[...]
    mu = jnp.sum(v, axis=-1, keepdims=True) * (1.0 / D_GMLP)
    vc = v - mu
    var = jnp.sum(vc * vc, axis=-1, keepdims=True) * (1.0 / D_GMLP)
    vn = vc * lax.rsqrt(var + EPS) * ln_g_ref[...] + ln_b_ref[...]
    if emit_v:
        v_ref[...] = vn
    vn_scr[...] = vn.astype(BF16)

    row = lax.broadcasted_iota(jnp.int32, (CHUNK, CHUNK), 0)
    col = lax.broadcasted_iota(jnp.int32, (CHUNK, CHUNK), 1)
    keep = (col <= row) & (col >= row - (row % period))

    for j in range(n_half):
        c0 = j * _GM_NC
        u = _gelu(_dot_cols(hn, w_in_ref, c0, _GM_NC) + b_in_ref[:, c0:c0 + _GM_NC])
        for gg in range(_GM_NC // GW_A):
            g = j * (_GM_NC // GW_A) + gg
            wsm = jnp.where(keep, ws_ref[g], 0.0).astype(BF16)
            bias = bs_ref[:, g:g + 1]
            for c in range(tm // CHUNK):
                r0 = c * CHUNK
                mixed = _dot_cols(wsm, vn_scr.at[r0:r0 + CHUNK, :], g * GW_A, GW_A) + bias
                p_scr[r0:r0 + CHUNK, g * GW_A:(g + 1) * GW_A] = (
                    u[r0:r0 + CHUNK, gg * GW_A:(gg + 1) * GW_A] * mixed).astype(BF16)

    m = _dot_cols(p_scr[...], w_out_ref, 0, D_MODEL)
    y_ref[...] = x + _rms(m, g_ref[1:2, :])


def _gmlp_layer(x, gains, w_in, b_in, ln_g, ln_b, ws, bs_t, w_out, *, tm, period, emit_v):
    rows = x.shape[0]
    assert rows % tm == 0 and tm % CHUNK == 0 and CHUNK % period == 0
    out_shape = [jax.ShapeDtypeStruct((rows, D_MODEL), F32)]
    out_specs = [pl.BlockSpec((tm, D_MODEL), lambda i: (i, 0))]
    if emit_v:
        out_shape.append(jax.ShapeDtypeStruct((rows, D_GMLP), F32))
        out_specs.append(pl.BlockSpec((tm, D_GMLP), lambda i: (i, 0)))
    res = pl.pallas_call(
        functools.partial(_gmlp_kernel, tm=tm, period=period, emit_v=emit_v),
        grid=(rows // tm,),
        in_specs=[
            pl.BlockSpec((tm, D_MODEL), lambda i: (i, 0)),
            _const_spec(gains.shape), _const_spec(w_in.shape), _const_spec(b_in.shape),
            _const_spec(ln_g.shape), _const_spec(ln_b.shape), _const_spec(ws.shape),
            _const_spec(bs_t.shape), _const_spec(w_out.shape),
        ],
        out_specs=out_specs,
        out_shape=out_shape,
        scratch_shapes=[pltpu.VMEM((tm, D_GMLP), F32), pltpu.VMEM((tm, D_GMLP), BF16),
                        pltpu.VMEM((tm, D_GMLP), BF16)],
        compiler_params=_params(),
        name="gmlp_mixer",
    )(x, gains, w_in, b_in, ln_g, ln_b, ws, bs_t, w_out)
    return res


_FF_NC = 256


def _ffn_kernel(x_ref, g_ref, w_in_ref, w_out_ref, y_ref, p_scr, *, d_ff):
    x = x_ref[...]
    hn = _rms(x, g_ref[2:3, :]).astype(BF16)
    for j in range(d_ff // _FF_NC):
        c0 = j * _FF_NC
        g = _dot(hn, w_in_ref[:, c0:c0 + _FF_NC])
        u = _dot(hn, w_in_ref[:, d_ff + c0:d_ff + c0 + _FF_NC])
        p_scr[:, c0:c0 + _FF_NC] = (g * _sigmoid(g) * u).astype(BF16)
    f = _dot_cols(p_scr[...], w_out_ref, 0, D_MODEL)
    y_ref[...] = x + _rms(f, g_ref[3:4, :])


def _ffn_layer(x, gains, w_in, w_out, *, tm):
    rows = x.shape[0]
    d_ff = w_out.shape[0]
    assert rows % tm == 0 and d_ff % _FF_NC == 0
    return pl.pallas_call(
        functools.partial(_ffn_kernel, d_ff=d_ff),
        grid=(rows // tm,),
        in_specs=[pl.BlockSpec((tm, D_MODEL), lambda i: (i, 0)),
                  _const_spec(gains.shape), _const_spec(w_in.shape), _const_spec(w_out.shape)],
        out_specs=pl.BlockSpec((tm, D_MODEL), lambda i: (i, 0)),
        out_shape=jax.ShapeDtypeStruct((rows, D_MODEL), F32),
        scratch_shapes=[pltpu.VMEM((tm, d_ff), BF16)],
        compiler_params=_params(),
        name="swiglu",
    )(x, gains, w_in, w_out)


_HIST = CONV_W - 1


def _scan_pitch(tt):
    return tt if (tt // SUBLANES) % 2 == 1 else tt + SUBLANES


def _rglru_kernel(x_ref, hist_ref, h0_ref, g_ref, w_in_ref, cw_ref, cb_ref, wax_ref, bax_ref,
                  lam_ref, w_out_ref, y_ref, hist_out_ref, h_out_ref,
                  xp_scr, gate_scr, a_scr, b_scr, h_scr, *, nb, tt):
    m_rows = nb * tt
    pitch = _scan_pitch(tt)
    lead = SUBLANES

    @pl.when(pl.program_id(0) == 0)
    def _():
        xp_scr[:, lead - _HIST:lead, :] = hist_ref[...]
        h_scr[...] = h0_ref[...]

    x = x_ref[...].reshape(m_rows, D_MODEL)
    hn = _rms(x, g_ref[0:1, :]).astype(BF16)
    gate_scr[...] = _gelu(_dot_cols(hn, w_in_ref, 0, D_RNN))
    xb = _dot_cols(hn, w_in_ref, D_RNN, D_RNN)
    xp_scr[:, lead:lead + tt, :] = xb.reshape(nb, tt, D_RNN)

    xc = cb_ref[...].reshape(1, 1, D_RNN)
    for k in range(CONV_W):
        s0 = lead - _HIST + k
        xc = xc + cw_ref[k:k + 1, :].reshape(1, 1, D_RNN) * xp_scr[:, s0:s0 + tt, :]
    new_hist = xp_scr[:, lead + tt - _HIST:lead + tt, :]
    hist_out_ref[...] = new_hist
    xp_scr[:, lead - _HIST:lead, :] = new_hist
    xc = xc.reshape(m_rows, D_RNN)
    xcb = xc.astype(BF16)

    lam = lam_ref[...]
    c_sp = LRU_C * (jnp.maximum(-lam, 0.0) + jnp.log1p(jnp.exp(-jnp.abs(lam))))

    for h in range(H_B):
        hs = slice(h * BW_B, (h + 1) * BW_B)
        ri = _dot(xcb[:, hs], wax_ref[h]) + bax_ref[h:h + 1, :]
        r = _sigmoid(ri[:, :BW_B])
        ig = _sigmoid(ri[:, BW_B:])
        nla = c_sp[:, hs] * r
        a = jnp.exp2(nla * (-_LOG2E))
        mult = jnp.sqrt(jnp.tanh(nla) * (1.0 + a * a))
        bb = mult * (ig * xc[:, hs])
        for b in range(nb):
            a_scr[h, b * pitch:b * pitch + tt, :] = a[b * tt:(b + 1) * tt, :]
            b_scr[h, b * pitch:b * pitch + tt, :] = bb[b * tt:(b + 1) * tt, :]

    for s in range(D_RNN // LANES):
        hcur = h_scr[:, s * LANES:(s + 1) * LANES]
        for t in range(tt):
            idx = pl.ds(t, nb, stride=pitch)
            hcur = a_scr[s, idx, :] * hcur + b_scr[s, idx, :]
            b_scr[s, idx, :] = hcur
        h_scr[:, s * LANES:(s + 1) * LANES] = hcur
    h_out_ref[...] = h_scr[...]

    hs_all = jnp.concatenate(
        [jnp.concatenate([b_scr[s, b * pitch:b * pitch + tt, :] for b in range(nb)], axis=0)
         for s in range(D_RNN // LANES)], axis=1)
    og = (hs_all * gate_scr[...]).astype(BF16)
    m = _dot_cols(og, w_out_ref, 0, D_MODEL)
    y_ref[...] = (x + _rms(m, g_ref[1:2, :])).reshape(nb, tt, D_MODEL)


def _rglru_layer(x, hist, h0, gains, w_in, cw, cb, wax, bax, lam, w_out, *, tt):
    nb, t_len, _ = x.shape
    assert t_len % tt == 0 and tt % SUBLANES == 0
    pitch = _scan_pitch(tt)
    m_rows = nb * tt
    return pl.pallas_call(
        functools.partial(_rglru_kernel, nb=nb, tt=tt),
        grid=(t_len // tt,),
        in_specs=[
            pl.BlockSpec((nb, tt, D_MODEL), lambda i: (0, i, 0)),
            _const_spec(hist.shape), _const_spec(h0.shape), _const_spec(gains.shape),
            _const_spec(w_in.shape), _const_spec(cw.shape), _const_spec(cb.shape),
            _const_spec(wax.shape), _const_spec(bax.shape), _const_spec(lam.shape),
            _const_spec(w_out.shape),
        ],
        out_specs=[
            pl.BlockSpec((nb, tt, D_MODEL), lambda i: (0, i, 0)),
            pl.BlockSpec((nb, _HIST, D_RNN), lambda i: (0, 0, 0)),
            pl.BlockSpec((nb, D_RNN), lambda i: (0, 0)),
        ],
        out_shape=[
            jax.ShapeDtypeStruct((nb, t_len, D_MODEL), F32),
            jax.ShapeDtypeStruct((nb, _HIST, D_RNN), F32),
            jax.ShapeDtypeStruct((nb, D_RNN), F32),
        ],
        scratch_shapes=[
            pltpu.VMEM((nb, tt + SUBLANES, D_RNN), F32),
            pltpu.VMEM((m_rows, D_RNN), F32),
            pltpu.VMEM((D_RNN // LANES, nb * pitch, LANES), F32),
            pltpu.VMEM((D_RNN // LANES, nb * pitch, LANES), F32),
            pltpu.VMEM((nb, D_RNN), F32),
        ],
        compiler_params=_params(),
        name="rglru_mixer",
    )(x, hist, h0, gains, w_in, cw, cb, wax, bax, lam, w_out)


def kernel(x_prompt, x_sample, state_conv, state_h, norm_gains, gm_w_in, gm_b_in, gm_ln_g, gm_ln_b,
           gm_w_s, gm_b_s, gm_w_out, rg_w_in, rg_conv_w, rg_conv_b, rg_w_a, rg_b_a, rg_w_x, rg_b_x,
           rg_lambda, rg_w_out, ffn_w_in, ffn_w_out):
    n_p, t_p, _ = x_prompt.shape
    n_s, t_s, _ = x_sample.shape
    assert CHUNK % t_s == 0 and t_p % CHUNK == 0

    gm_w_in_b = gm_w_in[0].astype(BF16)
    gm_w_out_b = gm_w_out[0].astype(BF16)
    gm_b_in2 = gm_b_in[0].reshape(1, -1)
    gm_ln_g2 = gm_ln_g[0].reshape(1, -1)
    gm_ln_b2 = gm_ln_b[0].reshape(1, -1)
    ws_p = gm_w_s[0]
    bs_p = gm_b_s[0].T
    rep = CHUNK // t_s
    ws_s = jnp.tile(gm_w_s[0][:, :t_s, :t_s], (1, rep, rep))
    bs_s = jnp.tile(gm_b_s[0][:, :t_s], (1, rep)).T
    rg_w_in_b = rg_w_in[0].astype(BF16)
    rg_w_out_b = rg_w_out[0].astype(BF16)
    rg_wax = jnp.concatenate([rg_w_a[0], rg_w_x[0]], axis=-1).astype(BF16)
    rg_bax = jnp.concatenate([rg_b_a[0], rg_b_x[0]], axis=-1)
    rg_cb = rg_conv_b[0].reshape(1, -1)
    rg_lam = rg_lambda[0].reshape(1, -1)
    ffn_w_in_b = [ffn_w_in[li].astype(BF16) for li in range(ffn_w_in.shape[0])]
    ffn_w_out_b = [ffn_w_out[li].astype(BF16) for li in range(ffn_w_out.shape[0])]

    def gmlp(x, ws, bs_t, period, emit_v, tm):
        return _gmlp_layer(x, norm_gains[0], gm_w_in_b, gm_b_in2, gm_ln_g2, gm_ln_b2, ws, bs_t,
                           gm_w_out_b, tm=tm, period=period, emit_v=emit_v)

    def ffn(x, li, tm):
        return _ffn_layer(x, norm_gains[li], ffn_w_in_b[li], ffn_w_out_b[li], tm=tm)

    def rglru(x, hist, h0, tt):
        return _rglru_layer(x, hist, h0, norm_gains[1], rg_w_in_b, rg_conv_w[0], rg_cb, rg_wax,
                            rg_bax, rg_lam, rg_w_out_b, tt=tt)

    xp = x_prompt.reshape(n_p * t_p, D_MODEL)
    (xp,) = gmlp(xp, ws_p, bs_p, CHUNK, False, 256)
    xp = ffn(xp, 0, 512)
    xp, p_conv, p_h = rglru(xp.reshape(n_p, t_p, D_MODEL),
                            jnp.zeros((n_p, _HIST, D_RNN), F32), jnp.zeros((n_p, D_RNN), F32), 64)
    y_prompt = ffn(xp.reshape(n_p * t_p, D_MODEL), 1, 512).reshape(n_p, t_p, D_MODEL)

    xs = x_sample.reshape(n_s * t_s, D_MODEL)
    xs, s_v = gmlp(xs, ws_s, bs_s, t_s, True, 512)
    xs = ffn(xs, 0, 1024)
    xs, s_conv, s_h = rglru(xs.reshape(n_s, t_s, D_MODEL), state_conv[0], state_h[0], t_s)
    y_sample = ffn(xs.reshape(n_s * t_s, D_MODEL), 1, 1024).reshape(n_s, t_s, D_MODEL)

    return (y_prompt, y_sample, p_conv[None], p_h[None], s_conv[None], s_h[None],
            s_v.reshape(1, n_s, t_s, D_GMLP))
```

```python
import functools
import math

import jax
import jax.numpy as jnp
from jax import lax
from jax.experimental import pallas as pl
from jax.experimental.pallas import tpu as pltpu

D_MODEL = 1024
CHUNK = 128
D_GMLP = 3 * D_MODEL
G_A = 8
GW_A = D_GMLP // G_A
D_RNN = D_MODEL
H_B = 8
BW_B = D_RNN // H_B
CONV_W = 4
LRU_C = 8.0
EPS = 1e-6

LANES = 128
SUBLANES = 8
MXU_N = 256
VMEM_LIMIT_BYTES = 56 * 2 ** 20

BF16 = jnp.bfloat16
F32 = jnp.float32

_LOG2E = math.log2(math.e)
_GELU_C = math.sqrt(2.0 / math.pi)
_GELU_K1 = -2.0 * _GELU_C * _LOG2E
_GELU_K3 = -2.0 * _GELU_C * 0.044715 * _LOG2E


def _gelu(x):
    return x * (1.0 / (1.0 + jnp.exp2(x * (_GELU_K1 + _GELU_K3 * (x * x)))))


def _sigmoid(x):
    return 1.0 / (1.0 + jnp.exp2(x * (-_LOG2E)))


def _rms(x, g):
    ms = jnp.sum(x * x, axis=-1, keepdims=True) * (1.0 / x.shape[-1])
    return x * lax.rsqrt(ms + EPS) * g


def _dot(a, b):
    return jnp.dot(a, b, preferred_element_type=F32)


def _dot_tiles(a, w_ref, j0, n):
    parts = [_dot(a, w_ref[j]) for j in range(j0, j0 + n)]
    return parts[0] if n == 1 else jnp.concatenate(parts, axis=1)


def _col_tiles(w):
    k, n = w.shape[-2:]
    w = w.reshape(w.shape[:-1] + (n // MXU_N, MXU_N))
    return jnp.swapaxes(w, -2, -3).astype(BF16)


def _const_spec(shape):
    return pl.BlockSpec(shape, lambda i: (0,) * len(shape), pipeline_mode=pl.Buffered(1))


def _layer_spec(shape, li):
    return pl.BlockSpec((None,) + tuple(shape[1:]), lambda i: (li,) + (0,) * (len(shape) - 1),
                        pipeline_mode=pl.Buffered(1))


def _params():
    return pltpu.CompilerParams(dimension_semantics=("arbitrary",),
                                vmem_limit_bytes=VMEM_LIMIT_BYTES)


_GM_NT = D_GMLP // MXU_N
_GM_UT = 3


def _gmlp_kernel(x_ref, g_ref, w_in_ref, b_in_ref, ln_g_ref, ln_b_ref, ws_ref, bs_ref, w_out_ref,
                 *rest, tm, period, emit_v):
    if emit_v:
        y_ref, v_ref, v_scr, vn_scr, p_scr = rest
    else:
        y_ref, v_scr, vn_scr, p_scr = rest
    x = x_ref[...]
    hn = _rms(x, g_ref[0:1, :]).astype(BF16)

    s1 = jnp.zeros((tm, LANES), F32)
    s2 = jnp.zeros((tm, LANES), F32)
    for j in range(_GM_NT):
        c0 = D_GMLP + j * MXU_N
        z = _gelu(_dot(hn, w_in_ref[_GM_NT + j]) + b_in_ref[:, c0:c0 + MXU_N])
        v_scr[:, j * MXU_N:(j + 1) * MXU_N] = z
        zz = z * z
        s1 = s1 + (z[:, :LANES] + z[:, LANES:])
        s2 = s2 + (zz[:, :LANES] + zz[:, LANES:])
    mu = jnp.sum(s1, axis=-1, keepdims=True) * (1.0 / D_GMLP)
    var = jnp.sum(s2, axis=-1, keepdims=True) * (1.0 / D_GMLP) - mu * mu
    rstd = lax.rsqrt(var + EPS)
    vn = (v_scr[...] - mu) * rstd * ln_g_ref[...] + ln_b_ref[...]
    if emit_v:
        v_ref[...] = vn
    vn_scr[...] = vn.astype(BF16)

    row = lax.broadcasted_iota(jnp.int32, (CHUNK, CHUNK), 0)
    col = lax.broadcasted_iota(jnp.int32, (CHUNK, CHUNK), 1)
    keep = (col <= row) & (col >= row - (row % period))

    gpc = _GM_UT * MXU_N // GW_A
    for j in range(_GM_NT // _GM_UT):
        c0 = j * _GM_UT * MXU_N
        u = _gelu(_dot_tiles(hn, w_in_ref, j * _GM_UT, _GM_UT) + b_in_ref[:, c0:c0 + _GM_UT * MXU_N])
        for gg in range(gpc):
            g = j * gpc + gg
            wsm = jnp.where(keep, ws_ref[g], 0.0).astype(BF16)
            bias = bs_ref[:, g:g + 1]
            for c in range(tm // CHUNK):
                r0 = c * CHUNK
                vg = vn_scr.at[r0:r0 + CHUNK, :]
                mixed = jnp.concatenate(
                    [_dot(wsm, vg[:, g * GW_A:g * GW_A + MXU_N]),
                     _dot(wsm, vg[:, g * GW_A + MXU_N:(g + 1) * GW_A])], axis=1) + bias
                p_scr[r0:r0 + CHUNK, g * GW_A:(g + 1) * GW_A] = (
                    u[r0:r0 + CHUNK, gg * GW_A:(gg + 1) * GW_A] * mixed).astype(BF16)

    m = _dot_tiles(p_scr[...], w_out_ref, 0, D_MODEL // MXU_N)
    y_ref[...] = x + _rms(m, g_ref[1:2, :])


def _gmlp_layer(x, gains, w_in, b_in, ln_g, ln_b, ws, bs_t, w_out, *, tm, period, emit_v):
    rows = x.shape[0]
    assert rows % tm == 0 and tm % CHUNK == 0 and CHUNK % period == 0
    out_shape = [jax.ShapeDtypeStruct((rows, D_MODEL), F32)]
    out_specs = [pl.BlockSpec((tm, D_MODEL), lambda i: (i, 0))]
    if emit_v:
        out_shape.append(jax.ShapeDtypeStruct((rows, D_GMLP), F32))
        out_specs.append(pl.BlockSpec((tm, D_GMLP), lambda i: (i, 0)))
    return pl.pallas_call(
        functools.partial(_gmlp_kernel, tm=tm, period=period, emit_v=emit_v),
        grid=(rows // tm,),
        in_specs=[
            pl.BlockSpec((tm, D_MODEL), lambda i: (i, 0)),
            _const_spec(gains.shape), _const_spec(w_in.shape), _const_spec(b_in.shape),
            _const_spec(ln_g.shape), _const_spec(ln_b.shape), _const_spec(ws.shape),
            _const_spec(bs_t.shape), _const_spec(w_out.shape),
        ],
        out_specs=out_specs,
        out_shape=out_shape,
        scratch_shapes=[pltpu.VMEM((tm, D_GMLP), F32), pltpu.VMEM((tm, D_GMLP), BF16),
                        pltpu.VMEM((tm, D_GMLP), BF16)],
        compiler_params=_params(),
        name="gmlp_mixer",
    )(x, gains, w_in, b_in, ln_g, ln_b, ws, bs_t, w_out)


def _ffn_kernel(x_ref, g_ref, w_in_ref, w_out_ref, y_ref, p_scr):
    nt = w_in_ref.shape[0] // 2
    x = x_ref[...]
    hn = _rms(x, g_ref[2:3, :]).astype(BF16)
    for j in range(nt):
        g = _dot(hn, w_in_ref[j])
        u = _dot(hn, w_in_ref[nt + j])
        p_scr[:, j * MXU_N:(j + 1) * MXU_N] = (g * _sigmoid(g) * u).astype(BF16)
    f = _dot_tiles(p_scr[...], w_out_ref, 0, D_MODEL // MXU_N)
    y_ref[...] = x + _rms(f, g_ref[3:4, :])


def _ffn_layer(x, gains, w_in, w_out, li, *, tm):
    rows = x.shape[0]
    d_ff = w_out.shape[2]
    assert rows % tm == 0
    return pl.pallas_call(
        _ffn_kernel,
        grid=(rows // tm,),
        in_specs=[pl.BlockSpec((tm, D_MODEL), lambda i: (i, 0)),
                  _layer_spec(gains.shape, li), _layer_spec(w_in.shape, li),
                  _layer_spec(w_out.shape, li)],
        out_specs=pl.BlockSpec((tm, D_MODEL), lambda i: (i, 0)),
        out_shape=jax.ShapeDtypeStruct((rows, D_MODEL), F32),
        scratch_shapes=[pltpu.VMEM((tm, d_ff), BF16)],
        compiler_params=_params(),
        name="swiglu",
    )(x, gains, w_in, w_out)


_HIST = CONV_W - 1
_NSLAB = D_RNN // LANES


def _seq_pitch(tt):
    return tt if (tt // SUBLANES) % 2 == 1 else tt + SUBLANES


def _rglru_kernel(x_ref, hist_ref, h0_ref, g_ref, w_in_ref, cw_ref, cb_ref, wax_ref, bax_ref,
                  lam_ref, w_out_ref, y_ref, hist_out_ref, h_out_ref,
                  slab_scr, xt_scr, gate_scr, a_scr, b_scr, h_scr, *, nb, tt):
    m_rows = nb * tt
    pitch = _seq_pitch(tt)
    nh = _HIST * nb
    nt = D_RNN // MXU_N

    @pl.when(pl.program_id(0) == 0)
    def _():
        xt_scr[0:nh, :] = hist_ref[...]
        h_scr[...] = h0_ref[...]

    x = x_ref[...].reshape(m_rows, D_MODEL)
    hn = _rms(x, g_ref[0:1, :]).astype(BF16)
    gate_scr[...] = _gelu(_dot_tiles(hn, w_in_ref, 0, nt))
    xb = _dot_tiles(hn, w_in_ref, nt, nt)

    for s in range(_NSLAB):
        for b in range(nb):
            slab_scr[s, b * pitch:b * pitch + tt, :] = xb[b * tt:(b + 1) * tt, s * LANES:(s + 1) * LANES]
    for t in range(tt):
        for s in range(_NSLAB):
            xt_scr[nh + t * nb:nh + (t + 1) * nb, s * LANES:(s + 1) * LANES] = (
                slab_scr[s, pl.ds(t, nb, stride=pitch), :])

    xc = cb_ref[...]
    for k in range(CONV_W):
        xc = xc + cw_ref[k:k + 1, :] * xt_scr[k * nb:k * nb + m_rows, :]
    new_hist = xt_scr[m_rows:m_rows + nh, :]
    hist_out_ref[...] = new_hist
    xt_scr[0:nh, :] = new_hist
    xcb = xc.astype(BF16)

    lam = lam_ref[...]
    c_sp = LRU_C * (jnp.maximum(-lam, 0.0) + jnp.log1p(jnp.exp(-jnp.abs(lam))))

    for h in range(H_B):
        hs = slice(h * BW_B, (h + 1) * BW_B)
        ri = _dot(xcb[:, hs], wax_ref[h]) + bax_ref[h:h + 1, :]
        r = _sigmoid(ri[:, :BW_B])
        ig = _sigmoid(ri[:, BW_B:])
        nla = c_sp[:, hs] * r
        a = jnp.exp2(nla * (-_LOG2E))
        mult = jnp.sqrt(jnp.tanh(nla) * (1.0 + a * a))
        a_scr[:, hs] = a
        b_scr[:, hs] = mult * (ig * xc[:, hs])

    for s in range(_NSLAB):
        ls = slice(s * LANES, (s + 1) * LANES)
        hcur = h_scr[:, ls]
        for t in range(tt):
            hcur = a_scr[t * nb:(t + 1) * nb, ls] * hcur + b_scr[t * nb:(t + 1) * nb, ls]
            slab_scr[s, pl.ds(t, nb, stride=pitch), :] = hcur
        h_scr[:, ls] = hcur
    h_out_ref[...] = h_scr[...]

    hs_all = jnp.concatenate(
        [jnp.concatenate([slab_scr[s, b * pitch:b * pitch + tt, :] for b in range(nb)], axis=0)
         for s in range(_NSLAB)], axis=1)
    og = (hs_all * gate_scr[...]).astype(BF16)
    m = _dot_tiles(og, w_out_ref, 0, D_MODEL // MXU_N)
    y_ref[...] = (x + _rms(m, g_ref[1:2, :])).reshape(nb, tt, D_MODEL)


def _rglru_layer(x, hist_tm, h0, gains, w_in, cw, cb, wax, bax, lam, w_out, *, tt):
    nb, t_len, _ = x.shape
    assert t_len % tt == 0 and tt % SUBLANES == 0 and nb % SUBLANES == 0
    pitch = _seq_pitch(tt)
    m_rows = nb * tt
    nh = _HIST * nb
    return pl.pallas_call(
        functools.partial(_rglru_kernel, nb=nb, tt=tt),
        grid=(t_len // tt,),
        in_specs=[
            pl.BlockSpec((nb, tt, D_MODEL), lambda i: (0, i, 0)),
            _const_spec(hist_tm.shape), _const_spec(h0.shape), _const_spec(gains.shape),
            _const_spec(w_in.shape), _const_spec(cw.shape), _const_spec(cb.shape),
            _const_spec(wax.shape), _const_spec(bax.shape), _const_spec(lam.shape),
            _const_spec(w_out.shape),
        ],
        out_specs=[
            pl.BlockSpec((nb, tt, D_MODEL), lambda i: (0, i, 0)),
            pl.BlockSpec((nh, D_RNN), lambda i: (0, 0)),
            pl.BlockSpec((nb, D_RNN), lambda i: (0, 0)),
        ],
        out_shape=[
            jax.ShapeDtypeStruct((nb, t_len, D_MODEL), F32),
            jax.ShapeDtypeStruct((nh, D_RNN), F32),
            jax.ShapeDtypeStruct((nb, D_RNN), F32),
        ],
        scratch_shapes=[
            pltpu.VMEM((_NSLAB, nb * pitch, LANES), F32),
            pltpu.VMEM((nh + m_rows, D_RNN), F32),
            pltpu.VMEM((m_rows, D_RNN), F32),
            pltpu.VMEM((m_rows, D_RNN), F32),
            pltpu.VMEM((m_rows, D_RNN), F32),
            pltpu.VMEM((nb, D_RNN), F32),
        ],
        compiler_params=_params(),
        name="rglru_mixer",
    )(x, hist_tm, h0, gains, w_in, cw, cb, wax, bax, lam, w_out)


def kernel(x_prompt, x_sample, state_conv, state_h, norm_gains, gm_w_in, gm_b_in, gm_ln_g, gm_ln_b,
           gm_w_s, gm_b_s, gm_w_out, rg_w_in, rg_conv_w, rg_conv_b, rg_w_a, rg_b_a, rg_w_x, rg_b_x,
           rg_lambda, rg_w_out, ffn_w_in, ffn_w_out):
    n_p, t_p, _ = x_prompt.shape
    n_s, t_s, _ = x_sample.shape
    assert CHUNK % t_s == 0 and t_p % CHUNK == 0

    gm_w_in_t = _col_tiles(gm_w_in[0])
    gm_w_out_t = _col_tiles(gm_w_out[0])
    gm_b_in2 = gm_b_in[0].reshape(1, -1)
    gm_ln_g2 = gm_ln_g[0].reshape(1, -1)
    gm_ln_b2 = gm_ln_b[0].reshape(1, -1)
    ws_p = gm_w_s[0]
    bs_p = gm_b_s[0].T
    rep = CHUNK // t_s
    ws_s = jnp.tile(gm_w_s[0][:, :t_s, :t_s], (1, rep, rep))
    bs_s = jnp.tile(gm_b_s[0][:, :t_s], (1, rep)).T
    rg_w_in_t = _col_tiles(rg_w_in[0])
    rg_w_out_t = _col_tiles(rg_w_out[0])
    rg_wax = jnp.concatenate([rg_w_a[0], rg_w_x[0]], axis=-1).astype(BF16)
    rg_bax = jnp.concatenate([rg_b_a[0], rg_b_x[0]], axis=-1)
    rg_cb = rg_conv_b[0].reshape(1, -1)
    rg_lam = rg_lambda[0].reshape(1, -1)
    ffn_w_in_t = _col_tiles(ffn_w_in)
    ffn_w_out_t = _col_tiles(ffn_w_out)

    def gmlp(x, ws, bs_t, period, emit_v, tm):
        return _gmlp_layer(x, norm_gains[0], gm_w_in_t, gm_b_in2, gm_ln_g2, gm_ln_b2, ws, bs_t,
                           gm_w_out_t, tm=tm, period=period, emit_v=emit_v)

    def ffn(x, li, tm):
        return _ffn_layer(x, norm_gains, ffn_w_in_t, ffn_w_out_t, li, tm=tm)

    def rglru(x, hist, h0, tt):
        nb = x.shape[0]
        hist_tm = jnp.swapaxes(hist, 0, 1).reshape(_HIST * nb, D_RNN)
        y, new_hist_tm, h_last = _rglru_layer(x, hist_tm, h0, norm_gains[1], rg_w_in_t, rg_conv_w[0],
                                              rg_cb, rg_wax, rg_bax, rg_lam, rg_w_out_t, tt=tt)
        return y, jnp.swapaxes(new_hist_tm.reshape(_HIST, nb, D_RNN), 0, 1), h_last

    xp = x_prompt.reshape(n_p * t_p, D_MODEL)
    (xp,) = gmlp(xp, ws_p, bs_p, CHUNK, False, 256)
    xp = ffn(xp, 0, 512)
    xp, p_conv, p_h = rglru(xp.reshape(n_p, t_p, D_MODEL),
                            jnp.zeros((n_p, _HIST, D_RNN), F32), jnp.zeros((n_p, D_RNN), F32), 64)
    y_prompt = ffn(xp.reshape(n_p * t_p, D_MODEL), 1, 512).reshape(n_p, t_p, D_MODEL)

    xs = x_sample.reshape(n_s * t_s, D_MODEL)
    xs, s_v = gmlp(xs, ws_s, bs_s, t_s, True, 512)
    xs = ffn(xs, 0, 1024)
    xs, s_conv, s_h = rglru(xs.reshape(n_s, t_s, D_MODEL), state_conv[0], state_h[0], t_s)
    y_sample = ffn(xs.reshape(n_s * t_s, D_MODEL), 1, 1024).reshape(n_s, t_s, D_MODEL)

    return (y_prompt, y_sample, p_conv[None], p_h[None], s_conv[None], s_h[None],
            s_v.reshape(1, n_s, t_s, D_GMLP))
```

```python
import functools
import math

import jax
import jax.numpy as jnp
from jax import lax
from jax.experimental import pallas as pl
from jax.experimental.pallas import tpu as pltpu

D_MODEL = 1024
CHUNK = 128
D_GMLP = 3 * D_MODEL
G_A = 8
GW_A = D_GMLP // G_A
D_RNN = D_MODEL
H_B = 8
BW_B = D_RNN // H_B
CONV_W = 4
LRU_C = 8.0
EPS = 1e-6

LANES = 128
SUBLANES = 8
MXU_N = 256
VMEM_LIMIT_BYTES = 56 * 2 ** 20

BF16 = jnp.bfloat16
F32 = jnp.float32

_LOG2E = math.log2(math.e)
_GELU_C = math.sqrt(2.0 / math.pi)
_GELU_K1 = -2.0 * _GELU_C * _LOG2E
_GELU_K3 = -2.0 * _GELU_C * 0.044715 * _LOG2E


def _gelu(x):
    return x * (1.0 / (1.0 + jnp.exp2(x * (_GELU_K1 + _GELU_K3 * (x * x)))))


def _sigmoid(x):
    return 1.0 / (1.0 + jnp.exp2(x * (-_LOG2E)))


def _rms(x, g):
    ms = jnp.sum(x * x, axis=-1, keepdims=True) * (1.0 / x.shape[-1])
    return x * lax.rsqrt(ms + EPS) * g


def _dot(a, b):
    return jnp.dot(a, b, preferred_element_type=F32)


def _dot_tiles(a, w_ref, j0, n):
    parts = [_dot(a, w_ref[j]) for j in range(j0, j0 + n)]
    return parts[0] if n == 1 else jnp.concatenate(parts, axis=1)


def _const_spec(shape):
    return pl.BlockSpec(shape, lambda i: (0,) * len(shape), pipeline_mode=pl.Buffered(1))


_CAST_COLS = 2 * MXU_N


def _n_cast(w_in, w_out):
    n = w_in.shape[-1] // _CAST_COLS
    assert w_in.shape[-1] == n * _CAST_COLS and w_out.shape[-2] % (n * 2 * SUBLANES) == 0
    assert w_out.shape[-1] % MXU_N == 0
    return n


def _cast_specs(w_in, w_out, li=None):
    n = _n_cast(w_in, w_out)
    lead, pre = ((), ()) if li is None else ((None,), (li,))
    step = lambda i: jnp.minimum(i, n - 1)
    return [pl.BlockSpec(lead + (w_in.shape[-2], _CAST_COLS), lambda i: pre + (0, step(i))),
            pl.BlockSpec(lead + (w_out.shape[-2] // n, w_out.shape[-1]), lambda i: pre + (step(i), 0))]


def _cast_scratch(w_in, w_out):
    return [pltpu.VMEM((w_in.shape[-1] // MXU_N, w_in.shape[-2], MXU_N), BF16),
            pltpu.VMEM((w_out.shape[-1] // MXU_N, w_out.shape[-2], MXU_N), BF16)]


def _cast_step(i, w_in_ref, w_out_ref, w_in_scr, w_out_scr):
    blk = w_in_ref[...]
    for t in range(_CAST_COLS // MXU_N):
        w_in_scr[(_CAST_COLS // MXU_N) * i + t] = blk[:, t * MXU_N:(t + 1) * MXU_N].astype(BF16)
    rb = w_out_ref.shape[0]
    r0 = pl.multiple_of(i * rb, rb)
    blk = w_out_ref[...]
    for t in range(w_out_scr.shape[0]):
        w_out_scr[t, pl.ds(r0, rb), :] = blk[:, t * MXU_N:(t + 1) * MXU_N].astype(BF16)


def _tile_index(i, first, count):
    return jnp.clip(i - first, 0, count - 1)


def _params():
    return pltpu.CompilerParams(dimension_semantics=("arbitrary",),
                                vmem_limit_bytes=VMEM_LIMIT_BYTES)


_GM_NT = D_GMLP // MXU_N
_GM_UT = 3


def _gmlp_kernel(x_ref, g_ref, w_in_f32, b_in_ref, ln_g_ref, ln_b_ref, ws_ref, bs_ref, w_out_f32,
                 *rest, n_cast, tm, period, emit_v):
    if emit_v:
        y_ref, v_ref, w_in_ref, w_out_ref, v_scr, vn_scr, p_scr = rest
    else:
        y_ref, w_in_ref, w_out_ref, v_scr, vn_scr, p_scr = rest
        v_ref = None
    i = pl.program_id(0)

    @pl.when(i < n_cast)
    def _():
        _cast_step(i, w_in_f32, w_out_f32, w_in_ref, w_out_ref)

    @pl.when(i >= n_cast)
    def _():
        _gmlp_tile(x_ref, g_ref, w_in_ref, b_in_ref, ln_g_ref, ln_b_ref, ws_ref, bs_ref, w_out_ref,
                   y_ref, v_ref, v_scr, vn_scr, p_scr, tm=tm, period=period)


def _gmlp_tile(x_ref, g_ref, w_in_ref, b_in_ref, ln_g_ref, ln_b_ref, ws_ref, bs_ref, w_out_ref,
               y_ref, v_ref, v_scr, vn_scr, p_scr, *, tm, period):
    emit_v = v_ref is not None
    x = x_ref[...]
    hn = _rms(x, g_ref[0:1, :]).astype(BF16)

    s1 = jnp.zeros((tm, LANES), F32)
    s2 = jnp.zeros((tm, LANES), F32)
    for j in range(_GM_NT):
        c0 = D_GMLP + j * MXU_N
        z = _gelu(_dot(hn, w_in_ref[_GM_NT + j]) + b_in_ref[:, c0:c0 + MXU_N])
        v_scr[:, j * MXU_N:(j + 1) * MXU_N] = z
        zz = z * z
        s1 = s1 + (z[:, :LANES] + z[:, LANES:])
        s2 = s2 + (zz[:, :LANES] + zz[:, LANES:])
    mu = jnp.sum(s1, axis=-1, keepdims=True) * (1.0 / D_GMLP)
    var = jnp.sum(s2, axis=-1, keepdims=True) * (1.0 / D_GMLP) - mu * mu
    rstd = lax.rsqrt(var + EPS)
    vn = (v_scr[...] - mu) * rstd * ln_g_ref[...] + ln_b_ref[...]
    if emit_v:
        v_ref[...] = vn
    vn_scr[...] = vn.astype(BF16)

    row = lax.broadcasted_iota(jnp.int32, (CHUNK, CHUNK), 0)
    col = lax.broadcasted_iota(jnp.int32, (CHUNK, CHUNK), 1)
    keep = (col <= row) & (col >= row - (row % period))

    gpc = _GM_UT * MXU_N // GW_A
    for j in range(_GM_NT // _GM_UT):
        c0 = j * _GM_UT * MXU_N
        u = _gelu(_dot_tiles(hn, w_in_ref, j * _GM_UT, _GM_UT) + b_in_ref[:, c0:c0 + _GM_UT * MXU_N])
        for gg in range(gpc):
            g = j * gpc + gg
            wsm = jnp.where(keep, ws_ref[g], 0.0).astype(BF16)
            bias = bs_ref[:, g:g + 1]
            for c in range(tm // CHUNK):
                r0 = c * CHUNK
                vg = vn_scr.at[r0:r0 + CHUNK, :]
                mixed = jnp.concatenate(
                    [_dot(wsm, vg[:, g * GW_A:g * GW_A + MXU_N]),
                     _dot(wsm, vg[:, g * GW_A + MXU_N:(g + 1) * GW_A])], axis=1) + bias
                p_scr[r0:r0 + CHUNK, g * GW_A:(g + 1) * GW_A] = (
                    u[r0:r0 + CHUNK, gg * GW_A:(gg + 1) * GW_A] * mixed).astype(BF16)

    m = _dot_tiles(p_scr[...], w_out_ref, 0, D_MODEL // MXU_N)
    y_ref[...] = x + _rms(m, g_ref[1:2, :])


def _gmlp_layer(x, gains, w_in, b_in, ln_g, ln_b, ws, bs_t, w_out, *, tm, period, emit_v):
    rows = x.shape[0]
    assert rows % tm == 0 and tm % CHUNK == 0 and CHUNK % period == 0
    n_cast = _n_cast(w_in, w_out)
    n_tiles = rows // tm
    row_tile = lambda i: (_tile_index(i, n_cast, n_tiles), 0)
    w_in_spec, w_out_spec = _cast_specs(w_in, w_out)
    out_shape = [jax.ShapeDtypeStruct((rows, D_MODEL), F32)]
    out_specs = [pl.BlockSpec((tm, D_MODEL), row_tile)]
    if emit_v:
        out_shape.append(jax.ShapeDtypeStruct((rows, D_GMLP), F32))
        out_specs.append(pl.BlockSpec((tm, D_GMLP), row_tile))
    return pl.pallas_call(
        functools.partial(_gmlp_kernel, n_cast=n_cast, tm=tm, period=period, emit_v=emit_v),
        grid=(n_cast + n_tiles,),
        in_specs=[
            pl.BlockSpec((tm, D_MODEL), row_tile),
            _const_spec(gains.shape), w_in_spec, _const_spec(b_in.shape),
            _const_spec(ln_g.shape), _const_spec(ln_b.shape), _const_spec(ws.shape),
            _const_spec(bs_t.shape), w_out_spec,
        ],
        out_specs=out_specs,
        out_shape=out_shape,
        scratch_shapes=_cast_scratch(w_in, w_out) + [
            pltpu.VMEM((tm, D_GMLP), F32), pltpu.VMEM((tm, D_GMLP), BF16),
            pltpu.VMEM((tm, D_GMLP), BF16)],
        compiler_params=_params(),
        name="gmlp_mixer",
    )(x, gains, w_in, b_in, ln_g, ln_b, ws, bs_t, w_out)


def _ffn_kernel(xp_ref, xs_ref, g_ref, w_in_f32, w_out_f32, yp_ref, ys_ref,
                w_in_ref, w_out_ref, p_scr, *, n_cast, n_p):
    i = pl.program_id(0)

    @pl.when(i < n_cast)
    def _():
        _cast_step(i, w_in_f32, w_out_f32, w_in_ref, w_out_ref)

    @pl.when(i >= n_cast)
    def _():
        is_prompt = i < n_cast + n_p
        nt = w_in_ref.shape[0] // 2
        x = jnp.where(is_prompt, xp_ref[...], xs_ref[...])
        hn = _rms(x, g_ref[2:3, :]).astype(BF16)
        for j in range(nt):
            g = _dot(hn, w_in_ref[j])
            u = _dot(hn, w_in_ref[nt + j])
            p_scr[:, j * MXU_N:(j + 1) * MXU_N] = (g * _sigmoid(g) * u).astype(BF16)
        f = _dot_tiles(p_scr[...], w_out_ref, 0, D_MODEL // MXU_N)
        y = x + _rms(f, g_ref[3:4, :])

        @pl.when(is_prompt)
        def _():
            yp_ref[...] = y

        @pl.when(jnp.logical_not(is_prompt))
        def _():
            ys_ref[...] = y


def _ffn_layer(xp, xs, gains, w_in, w_out, li, *, tm):
    assert xp.shape[0] % tm == 0 and xs.shape[0] % tm == 0
    n_cast = _n_cast(w_in, w_out)
    n_p, n_s = xp.shape[0] // tm, xs.shape[0] // tm
    p_tile = lambda i: (_tile_index(i, n_cast, n_p), 0)
    s_tile = lambda i: (_tile_index(i, n_cast + n_p, n_s), 0)
    return pl.pallas_call(
        functools.partial(_ffn_kernel, n_cast=n_cast, n_p=n_p),
        grid=(n_cast + n_p + n_s,),
        in_specs=[pl.BlockSpec((tm, D_MODEL), p_tile), pl.BlockSpec((tm, D_MODEL), s_tile),
                  pl.BlockSpec((None,) + gains.shape[1:], lambda i: (li, 0, 0),
                               pipeline_mode=pl.Buffered(1))] + _cast_specs(w_in, w_out, li),
        out_specs=[pl.BlockSpec((tm, D_MODEL), p_tile), pl.BlockSpec((tm, D_MODEL), s_tile)],
        out_shape=[jax.ShapeDtypeStruct(xp.shape, F32), jax.ShapeDtypeStruct(xs.shape, F32)],
        scratch_shapes=_cast_scratch(w_in, w_out) + [pltpu.VMEM((tm, w_out.shape[-2]), BF16)],
        compiler_params=_params(),
        name="swiglu",
    )(xp, xs, gains, w_in, w_out)


_HIST = CONV_W - 1
_NSLAB = D_RNN // LANES


def _seq_pitch(tt):
    return tt if (tt // SUBLANES) % 2 == 1 else tt + SUBLANES


def _rglru_kernel(x_ref, hist_ref, h0_ref, g_ref, w_in_f32, cw_ref, cb_ref, wax_ref, bax_ref,
                  lam_ref, w_out_f32, y_ref, hist_out_ref, h_out_ref,
                  w_in_ref, w_out_ref, slab_scr, xt_scr, gate_scr, a_scr, b_scr, h_scr,
                  *, n_cast, nb, tt):
    i = pl.program_id(0)

    @pl.when(i == 0)
    def _():
        xt_scr[0:_HIST * nb, :] = hist_ref[...]
        h_scr[...] = h0_ref[...]

    @pl.when(i < n_cast)
    def _():
        _cast_step(i, w_in_f32, w_out_f32, w_in_ref, w_out_ref)

    @pl.when(i >= n_cast)
    def _():
        _rglru_tile(x_ref, g_ref, w_in_ref, cw_ref, cb_ref, wax_ref, bax_ref, lam_ref, w_out_ref,
                    y_ref, hist_out_ref, h_out_ref, slab_scr, xt_scr, gate_scr, a_scr, b_scr, h_scr,
                    nb=nb, tt=tt)


def _rglru_tile(x_ref, g_ref, w_in_ref, cw_ref, cb_ref, wax_ref, bax_ref, lam_ref, w_out_ref,
                y_ref, hist_out_ref, h_out_ref, slab_scr, xt_scr, gate_scr, a_scr, b_scr, h_scr,
                *, nb, tt):
    m_rows = nb * tt
    pitch = _seq_pitch(tt)
    nh = _HIST * nb
    nt = D_RNN // MXU_N

    x = x_ref[...].reshape(m_rows, D_MODEL)
    hn = _rms(x, g_ref[0:1, :]).astype(BF16)
    gate_scr[...] = _gelu(_dot_tiles(hn, w_in_ref, 0, nt))
    xb = _dot_tiles(hn, w_in_ref, nt, nt)

    for s in range(_NSLAB):
        for b in range(nb):
            slab_scr[s, b * pitch:b * pitch + tt, :] = xb[b * tt:(b + 1) * tt, s * LANES:(s + 1) * LANES]
    for t in range(tt):
        for s in range(_NSLAB):
            xt_scr[nh + t * nb:nh + (t + 1) * nb, s * LANES:(s + 1) * LANES] = (
                slab_scr[s, pl.ds(t, nb, stride=pitch), :])

    xc = cb_ref[...]
    for k in range(CONV_W):
        xc = xc + cw_ref[k:k + 1, :] * xt_scr[k * nb:k * nb + m_rows, :]
    new_hist = xt_scr[m_rows:m_rows + nh, :]
    hist_out_ref[...] = new_hist
    xt_scr[0:nh, :] = new_hist
    xcb = xc.astype(BF16)

    lam = lam_ref[...]
    c_sp = LRU_C * (jnp.maximum(-lam, 0.0) + jnp.log1p(jnp.exp(-jnp.abs(lam))))

    for h in range(H_B):
        hs = slice(h * BW_B, (h + 1) * BW_B)
        ri = _dot(xcb[:, hs], wax_ref[h]) + bax_ref[h:h + 1, :]
        r = _sigmoid(ri[:, :BW_B])
        ig = _sigmoid(ri[:, BW_B:])
        nla = c_sp[:, hs] * r
        a = jnp.exp2(nla * (-_LOG2E))
        mult = jnp.sqrt(jnp.tanh(nla) * (1.0 + a * a))
        a_scr[:, hs] = a
        b_scr[:, hs] = mult * (ig * xc[:, hs])

    for s in range(_NSLAB):
        ls = slice(s * LANES, (s + 1) * LANES)
        hcur = h_scr[:, ls]
        for t in range(tt):
            hcur = a_scr[t * nb:(t + 1) * nb, ls] * hcur + b_scr[t * nb:(t + 1) * nb, ls]
            slab_scr[s, pl.ds(t, nb, stride=pitch), :] = hcur
        h_scr[:, ls] = hcur
    h_out_ref[...] = h_scr[...]

    hs_all = jnp.concatenate(
        [jnp.concatenate([slab_scr[s, b * pitch:b * pitch + tt, :] for b in range(nb)], axis=0)
         for s in range(_NSLAB)], axis=1)
    og = (hs_all * gate_scr[...]).astype(BF16)
    m = _dot_tiles(og, w_out_ref, 0, D_MODEL // MXU_N)
    y_ref[...] = (x + _rms(m, g_ref[1:2, :])).reshape(nb, tt, D_MODEL)


def _rglru_layer(x, hist_tm, h0, gains, w_in, cw, cb, wax, bax, lam, w_out, *, tt):
    nb, t_len, _ = x.shape
    assert t_len % tt == 0 and tt % SUBLANES == 0 and nb % SUBLANES == 0
    pitch = _seq_pitch(tt)
    m_rows = nb * tt
    nh = _HIST * nb
    n_cast = _n_cast(w_in, w_out)
    n_tiles = t_len // tt
    time_tile = lambda i: (0, _tile_index(i, n_cast, n_tiles), 0)
    w_in_spec, w_out_spec = _cast_specs(w_in, w_out)
    return pl.pallas_call(
        functools.partial(_rglru_kernel, n_cast=n_cast, nb=nb, tt=tt),
        grid=(n_cast + n_tiles,),
        in_specs=[
            pl.BlockSpec((nb, tt, D_MODEL), time_tile),
            _const_spec(hist_tm.shape), _const_spec(h0.shape), _const_spec(gains.shape),
            w_in_spec, _const_spec(cw.shape), _const_spec(cb.shape),
            _const_spec(wax.shape), _const_spec(bax.shape), _const_spec(lam.shape),
            w_out_spec,
        ],
        out_specs=[
            pl.BlockSpec((nb, tt, D_MODEL), time_tile),
            pl.BlockSpec((nh, D_RNN), lambda i: (0, 0)),
            pl.BlockSpec((nb, D_RNN), lambda i: (0, 0)),
        ],
        out_shape=[
            jax.ShapeDtypeStruct((nb, t_len, D_MODEL), F32),
            jax.ShapeDtypeStruct((nh, D_RNN), F32),
            jax.ShapeDtypeStruct((nb, D_RNN), F32),
        ],
        scratch_shapes=_cast_scratch(w_in, w_out) + [
            pltpu.VMEM((_NSLAB, nb * pitch, LANES), F32),
            pltpu.VMEM((nh + m_rows, D_RNN), F32),
            pltpu.VMEM((m_rows, D_RNN), F32),
            pltpu.VMEM((m_rows, D_RNN), F32),
            pltpu.VMEM((m_rows, D_RNN), F32),
            pltpu.VMEM((nb, D_RNN), F32),
        ],
        compiler_params=_params(),
        name="rglru_mixer",
    )(x, hist_tm, h0, gains, w_in, cw, cb, wax, bax, lam, w_out)


def kernel(x_prompt, x_sample, state_conv, state_h, norm_gains, gm_w_in, gm_b_in, gm_ln_g, gm_ln_b,
           gm_w_s, gm_b_s, gm_w_out, rg_w_in, rg_conv_w, rg_conv_b, rg_w_a, rg_b_a, rg_w_x, rg_b_x,
           rg_lambda, rg_w_out, ffn_w_in, ffn_w_out):
    n_p, t_p, _ = x_prompt.shape
    n_s, t_s, _ = x_sample.shape
    assert CHUNK % t_s == 0 and t_p % CHUNK == 0

    gm_b_in2 = gm_b_in[0].reshape(1, -1)
    gm_ln_g2 = gm_ln_g[0].reshape(1, -1)
    gm_ln_b2 = gm_ln_b[0].reshape(1, -1)
    ws_p = gm_w_s[0]
    bs_p = gm_b_s[0].T
    rep = CHUNK // t_s
    ws_s = jnp.tile(gm_w_s[0][:, :t_s, :t_s], (1, rep, rep))
    bs_s = jnp.tile(gm_b_s[0][:, :t_s], (1, rep)).T
    rg_wax = jnp.concatenate([rg_w_a[0], rg_w_x[0]], axis=-1).astype(BF16)
    rg_bax = jnp.concatenate([rg_b_a[0], rg_b_x[0]], axis=-1)
    rg_cb = rg_conv_b[0].reshape(1, -1)
    rg_lam = rg_lambda[0].reshape(1, -1)

    def gmlp(x, ws, bs_t, period, emit_v, tm):
        return _gmlp_layer(x, norm_gains[0], gm_w_in[0], gm_b_in2, gm_ln_g2, gm_ln_b2, ws, bs_t,
                           gm_w_out[0], tm=tm, period=period, emit_v=emit_v)

    def ffn(xp, xs, li):
        return _ffn_layer(xp, xs, norm_gains, ffn_w_in, ffn_w_out, li, tm=512)

    def rglru(x, hist, h0, tt):
        nb = x.shape[0]
        hist_tm = jnp.swapaxes(hist, 0, 1).reshape(_HIST * nb, D_RNN)
        y, new_hist_tm, h_last = _rglru_layer(x, hist_tm, h0, norm_gains[1], rg_w_in[0], rg_conv_w[0],
                                              rg_cb, rg_wax, rg_bax, rg_lam, rg_w_out[0], tt=tt)
        return y, jnp.swapaxes(new_hist_tm.reshape(_HIST, nb, D_RNN), 0, 1), h_last

    xp = x_prompt.reshape(n_p * t_p, D_MODEL)
    xs = x_sample.reshape(n_s * t_s, D_MODEL)
    (xp,) = gmlp(xp, ws_p, bs_p, CHUNK, False, 256)
    xs, s_v = gmlp(xs, ws_s, bs_s, t_s, True, 256)
    xp, xs = ffn(xp, xs, 0)
    xp, p_conv, p_h = rglru(xp.reshape(n_p, t_p, D_MODEL),
                            jnp.zeros((n_p, _HIST, D_RNN), F32), jnp.zeros((n_p, D_RNN), F32), 64)
    xs, s_conv, s_h = rglru(xs.reshape(n_s, t_s, D_MODEL), state_conv[0], state_h[0], t_s)
    y_prompt, y_sample = ffn(xp.reshape(n_p * t_p, D_MODEL), xs.reshape(n_s * t_s, D_MODEL), 1)

    return (y_prompt.reshape(n_p, t_p, D_MODEL), y_sample.reshape(n_s, t_s, D_MODEL),
            p_conv[None], p_h[None], s_conv[None], s_h[None], s_v.reshape(1, n_s, t_s, D_GMLP))
```

```python
import functools
import math

import jax
import jax.numpy as jnp
from jax import lax
from jax.experimental import pallas as pl
from jax.experimental.pallas import tpu as pltpu

D_MODEL = 1024
CHUNK = 128
D_GMLP = 3 * D_MODEL
G_A = 8
GW_A = D_GMLP // G_A
D_RNN = D_MODEL
H_B = 8
BW_B = D_RNN // H_B
CONV_W = 4
LRU_C = 8.0
EPS = 1e-6

LANES = 128
SUBLANES = 8
MXU_N = 256
VMEM_LIMIT_BYTES = 56 * 2 ** 20

BF16 = jnp.bfloat16
F32 = jnp.float32

_LOG2E = math.log2(math.e)
_GELU_C = math.sqrt(2.0 / math.pi)
_GELU_K1 = -2.0 * _GELU_C * _LOG2E
_GELU_K3 = -2.0 * _GELU_C * 0.044715 * _LOG2E


def _gelu(x):
    return x * (1.0 / (1.0 + jnp.exp2(x * (_GELU_K1 + _GELU_K3 * (x * x)))))


def _sigmoid(x):
    return 1.0 / (1.0 + jnp.exp2(x * (-_LOG2E)))


def _rms(x, g):
    ms = jnp.sum(x * x, axis=-1, keepdims=True) * (1.0 / x.shape[-1])
    return x * lax.rsqrt(ms + EPS) * g


def _dot(a, b):
    return jnp.dot(a, b, preferred_element_type=F32)


def _dot_tiles(a, w_ref, j0, n):
    parts = [_dot(a, w_ref[j]) for j in range(j0, j0 + n)]
    return parts[0] if n == 1 else jnp.concatenate(parts, axis=1)


def _const_spec(shape):
    return pl.BlockSpec(shape, lambda i: (0,) * len(shape), pipeline_mode=pl.Buffered(1))


_CAST_COLS = 2 * MXU_N


def _n_cast(w_in, w_out):
    n = w_in.shape[-1] // _CAST_COLS
    assert w_in.shape[-1] == n * _CAST_COLS and w_out.shape[-2] % (n * 2 * SUBLANES) == 0
    assert w_out.shape[-1] % MXU_N == 0
    return n


def _cast_specs(w_in, w_out, li=None):
    n = _n_cast(w_in, w_out)
    lead, pre = ((), ()) if li is None else ((None,), (li,))
    step = lambda i: jnp.minimum(i, n - 1)
    return [pl.BlockSpec(lead + (w_in.shape[-2], _CAST_COLS), lambda i: pre + (0, step(i))),
            pl.BlockSpec(lead + (w_out.shape[-2] // n, w_out.shape[-1]), lambda i: pre + (step(i), 0))]


def _cast_scratch(w_in, w_out):
    return [pltpu.VMEM((w_in.shape[-1] // MXU_N, w_in.shape[-2], MXU_N), BF16),
            pltpu.VMEM((w_out.shape[-1] // MXU_N, w_out.shape[-2], MXU_N), BF16)]


def _cast_step(i, w_in_ref, w_out_ref, w_in_scr, w_out_scr):
    blk = w_in_ref[...]
    for t in range(_CAST_COLS // MXU_N):
        w_in_scr[(_CAST_COLS // MXU_N) * i + t] = blk[:, t * MXU_N:(t + 1) * MXU_N].astype(BF16)
    rb = w_out_ref.shape[0]
    r0 = pl.multiple_of(i * rb, rb)
    blk = w_out_ref[...]
    for t in range(w_out_scr.shape[0]):
        w_out_scr[t, pl.ds(r0, rb), :] = blk[:, t * MXU_N:(t + 1) * MXU_N].astype(BF16)


def _tile_index(i, first, count):
    return jnp.clip(i - first, 0, count - 1)


def _params():
    return pltpu.CompilerParams(dimension_semantics=("arbitrary",),
                                vmem_limit_bytes=VMEM_LIMIT_BYTES)


_GM_NT = D_GMLP // MXU_N
_GM_UT = 3


def _gmlp_kernel(xp_ref, xs_ref, g_ref, w_in_f32, b_in_ref, ln_g_ref, ln_b_ref, ws_ref, bs_ref,
                 w_out_f32, yp_ref, ys_ref, v_ref, w_in_ref, w_out_ref, v_scr, vn_scr, p_scr,
                 *, n_cast, n_p, tm, period_s):
    i = pl.program_id(0)

    @pl.when(i < n_cast)
    def _():
        _cast_step(i, w_in_f32, w_out_f32, w_in_ref, w_out_ref)

    @pl.when(i >= n_cast)
    def _():
        _gmlp_tile(i < n_cast + n_p, xp_ref, xs_ref, g_ref, w_in_ref, b_in_ref, ln_g_ref, ln_b_ref,
                   ws_ref, bs_ref, w_out_ref, yp_ref, ys_ref, v_ref, v_scr, vn_scr, p_scr,
                   tm=tm, period_s=period_s)


def _gmlp_tile(is_prompt, xp_ref, xs_ref, g_ref, w_in_ref, b_in_ref, ln_g_ref, ln_b_ref, ws_ref,
               bs_ref, w_out_ref, yp_ref, ys_ref, v_ref, v_scr, vn_scr, p_scr, *, tm, period_s):
    is_sample = jnp.logical_not(is_prompt)
    x = jnp.where(is_prompt, xp_ref[...], xs_ref[...])
    hn = _rms(x, g_ref[0:1, :]).astype(BF16)

    s1 = jnp.zeros((tm, LANES), F32)
    s2 = jnp.zeros((tm, LANES), F32)
    for j in range(_GM_NT):
        c0 = D_GMLP + j * MXU_N
        z = _gelu(_dot(hn, w_in_ref[_GM_NT + j]) + b_in_ref[:, c0:c0 + MXU_N])
        v_scr[:, j * MXU_N:(j + 1) * MXU_N] = z
        zz = z * z
        s1 = s1 + (z[:, :LANES] + z[:, LANES:])
        s2 = s2 + (zz[:, :LANES] + zz[:, LANES:])
    mu = jnp.sum(s1, axis=-1, keepdims=True) * (1.0 / D_GMLP)
    var = jnp.sum(s2, axis=-1, keepdims=True) * (1.0 / D_GMLP) - mu * mu
    rstd = lax.rsqrt(var + EPS)
    vn = (v_scr[...] - mu) * rstd * ln_g_ref[...] + ln_b_ref[...]

    @pl.when(is_sample)
    def _():
        v_ref[...] = vn

    vn_scr[...] = vn.astype(BF16)

    row = lax.broadcasted_iota(jnp.int32, (CHUNK, CHUNK), 0)
    col = lax.broadcasted_iota(jnp.int32, (CHUNK, CHUNK), 1)
    first = jnp.where(is_prompt, 0, row - (row % period_s))
    keep = (col <= row) & (col >= first)

    gpc = _GM_UT * MXU_N // GW_A
    for j in range(_GM_NT // _GM_UT):
        c0 = j * _GM_UT * MXU_N
        u = _gelu(_dot_tiles(hn, w_in_ref, j * _GM_UT, _GM_UT) + b_in_ref[:, c0:c0 + _GM_UT * MXU_N])
        for gg in range(gpc):
            g = j * gpc + gg
            wsm = jnp.where(keep, ws_ref[g], 0.0).astype(BF16)
            bias = bs_ref[:, g:g + 1]
            for c in range(tm // CHUNK):
                r0 = c * CHUNK
                vg = vn_scr.at[r0:r0 + CHUNK, :]
                mixed = jnp.concatenate(
                    [_dot(wsm, vg[:, g * GW_A:g * GW_A + MXU_N]),
                     _dot(wsm, vg[:, g * GW_A + MXU_N:(g + 1) * GW_A])], axis=1) + bias
                p_scr[r0:r0 + CHUNK, g * GW_A:(g + 1) * GW_A] = (
                    u[r0:r0 + CHUNK, gg * GW_A:(gg + 1) * GW_A] * mixed).astype(BF16)

    m = _dot_tiles(p_scr[...], w_out_ref, 0, D_MODEL // MXU_N)
    y = x + _rms(m, g_ref[1:2, :])

    @pl.when(is_prompt)
    def _():
        yp_ref[...] = y

    @pl.when(is_sample)
    def _():
        ys_ref[...] = y


def _gmlp_layer(xp, xs, gains, w_in, b_in, ln_g, ln_b, ws2, bs2, w_out, *, tm, period_s):
    assert xp.shape[0] % tm == 0 and xs.shape[0] % tm == 0
    assert tm % CHUNK == 0 and CHUNK % period_s == 0
    n_cast = _n_cast(w_in, w_out)
    n_p, n_s = xp.shape[0] // tm, xs.shape[0] // tm
    p_tile = lambda i: (_tile_index(i, n_cast, n_p), 0)
    s_tile = lambda i: (_tile_index(i, n_cast + n_p, n_s), 0)
    which = lambda i: jnp.where(i < n_cast + n_p, 0, 1)
    w_in_spec, w_out_spec = _cast_specs(w_in, w_out)
    return pl.pallas_call(
        functools.partial(_gmlp_kernel, n_cast=n_cast, n_p=n_p, tm=tm, period_s=period_s),
        grid=(n_cast + n_p + n_s,),
        in_specs=[
            pl.BlockSpec((tm, D_MODEL), p_tile), pl.BlockSpec((tm, D_MODEL), s_tile),
            _const_spec(gains.shape), w_in_spec, _const_spec(b_in.shape),
            _const_spec(ln_g.shape), _const_spec(ln_b.shape),
            pl.BlockSpec((None,) + ws2.shape[1:], lambda i: (which(i), 0, 0, 0)),
            pl.BlockSpec((None,) + bs2.shape[1:], lambda i: (which(i), 0, 0)),
            w_out_spec,
        ],
        out_specs=[pl.BlockSpec((tm, D_MODEL), p_tile), pl.BlockSpec((tm, D_MODEL), s_tile),
                   pl.BlockSpec((tm, D_GMLP), s_tile)],
        out_shape=[jax.ShapeDtypeStruct(xp.shape, F32), jax.ShapeDtypeStruct(xs.shape, F32),
                   jax.ShapeDtypeStruct((xs.shape[0], D_GMLP), F32)],
        scratch_shapes=_cast_scratch(w_in, w_out) + [
            pltpu.VMEM((tm, D_GMLP), F32), pltpu.VMEM((tm, D_GMLP), BF16),
            pltpu.VMEM((tm, D_GMLP), BF16)],
        compiler_params=_params(),
        name="gmlp_mixer",
    )(xp, xs, gains, w_in, b_in, ln_g, ln_b, ws2, bs2, w_out)


def _ffn_kernel(xp_ref, xs_ref, g_ref, w_in_f32, w_out_f32, yp_ref, ys_ref,
                w_in_ref, w_out_ref, p_scr, *, n_cast, n_p):
    i = pl.program_id(0)

    @pl.when(i < n_cast)
    def _():
        _cast_step(i, w_in_f32, w_out_f32, w_in_ref, w_out_ref)

    @pl.when(i >= n_cast)
    def _():
        is_prompt = i < n_cast + n_p
        nt = w_in_ref.shape[0] // 2
        x = jnp.where(is_prompt, xp_ref[...], xs_ref[...])
        hn = _rms(x, g_ref[2:3, :]).astype(BF16)
        for j in range(nt):
            g = _dot(hn, w_in_ref[j])
            u = _dot(hn, w_in_ref[nt + j])
            p_scr[:, j * MXU_N:(j + 1) * MXU_N] = (g * _sigmoid(g) * u).astype(BF16)
        f = _dot_tiles(p_scr[...], w_out_ref, 0, D_MODEL // MXU_N)
        y = x + _rms(f, g_ref[3:4, :])

        @pl.when(is_prompt)
        def _():
            yp_ref[...] = y

        @pl.when(jnp.logical_not(is_prompt))
        def _():
            ys_ref[...] = y


def _ffn_layer(xp, xs, gains, w_in, w_out, li, *, tm):
    assert xp.shape[0] % tm == 0 and xs.shape[0] % tm == 0
    n_cast = _n_cast(w_in, w_out)
    n_p, n_s = xp.shape[0] // tm, xs.shape[0] // tm
    p_tile = lambda i: (_tile_index(i, n_cast, n_p), 0)
    s_tile = lambda i: (_tile_index(i, n_cast + n_p, n_s), 0)
    return pl.pallas_call(
        functools.partial(_ffn_kernel, n_cast=n_cast, n_p=n_p),
        grid=(n_cast + n_p + n_s,),
        in_specs=[pl.BlockSpec((tm, D_MODEL), p_tile), pl.BlockSpec((tm, D_MODEL), s_tile),
                  pl.BlockSpec((None,) + gains.shape[1:], lambda i: (li, 0, 0),
                               pipeline_mode=pl.Buffered(1))] + _cast_specs(w_in, w_out, li),
        out_specs=[pl.BlockSpec((tm, D_MODEL), p_tile), pl.BlockSpec((tm, D_MODEL), s_tile)],
        out_shape=[jax.ShapeDtypeStruct(xp.shape, F32), jax.ShapeDtypeStruct(xs.shape, F32)],
        scratch_shapes=_cast_scratch(w_in, w_out) + [pltpu.VMEM((tm, w_out.shape[-2]), BF16)],
        compiler_params=_params(),
        name="swiglu",
    )(xp, xs, gains, w_in, w_out)


_HIST = CONV_W - 1
_NSLAB = D_RNN // LANES


def _seq_pitch(tt):
    return tt if (tt // SUBLANES) % 2 == 1 else tt + SUBLANES


def _rglru_kernel(x_ref, hist_ref, h0_ref, g_ref, w_in_f32, cw_ref, cb_ref, wax_ref, bax_ref,
                  lam_ref, w_out_f32, y_ref, hist_out_ref, h_out_ref,
                  w_in_ref, w_out_ref, slab_in, slab_out, xt_scr, gate_scr, a_scr, b_scr, h_scr,
                  *, n_cast, nb, tt, n_sub):
    i = pl.program_id(0)
    m_rows = nb * tt
    pitch = _seq_pitch(tt)
    nh = _HIST * nb
    nt = D_RNN // MXU_N

    @pl.when(i == 0)
    def _():
        xt_scr[0:nh, :] = hist_ref[...]
        h_scr[...] = h0_ref[...]

    @pl.when(i < n_cast)
    def _():
        _cast_step(i, w_in_f32, w_out_f32, w_in_ref, w_out_ref)

    def front(k):
        rows = slice(k * m_rows, (k + 1) * m_rows)
        x = x_ref[:, k * tt:(k + 1) * tt, :].reshape(m_rows, D_MODEL)
        hn = _rms(x, g_ref[0:1, :]).astype(BF16)
        gate_scr[rows, :] = _gelu(_dot_tiles(hn, w_in_ref, 0, nt))
        xb = _dot_tiles(hn, w_in_ref, nt, nt)

        for s in range(_NSLAB):
            for b in range(nb):
                slab_in[k, s, b * pitch:b * pitch + tt, :] = (
                    xb[b * tt:(b + 1) * tt, s * LANES:(s + 1) * LANES])
        r0 = nh + k * m_rows
        for t in range(tt):
            for s in range(_NSLAB):
                xt_scr[r0 + t * nb:r0 + (t + 1) * nb, s * LANES:(s + 1) * LANES] = (
                    slab_in[k, s, pl.ds(t, nb, stride=pitch), :])

        xc = cb_ref[...]
        for kk in range(CONV_W):
            c0 = k * m_rows + kk * nb
            xc = xc + cw_ref[kk:kk + 1, :] * xt_scr[c0:c0 + m_rows, :]
        xcb = xc.astype(BF16)

        lam = lam_ref[...]
        c_sp = LRU_C * (jnp.maximum(-lam, 0.0) + jnp.log1p(jnp.exp(-jnp.abs(lam))))
        for h in range(H_B):
            hs = slice(h * BW_B, (h + 1) * BW_B)
            ri = _dot(xcb[:, hs], wax_ref[h]) + bax_ref[h:h + 1, :]
            r = _sigmoid(ri[:, :BW_B])
            ig = _sigmoid(ri[:, BW_B:])
            nla = c_sp[:, hs] * r
            a = jnp.exp2(nla * (-_LOG2E))
            mult = jnp.sqrt(jnp.tanh(nla) * (1.0 + a * a))
            a_scr[rows, hs] = a
            b_scr[rows, hs] = mult * (ig * xc[:, hs])

    def back(k):
        r0 = k * m_rows
        for s in range(_NSLAB):
            ls = slice(s * LANES, (s + 1) * LANES)
            hcur = h_scr[:, ls]
            for t in range(tt):
                tr = slice(r0 + t * nb, r0 + (t + 1) * nb)
                hcur = a_scr[tr, ls] * hcur + b_scr[tr, ls]
                slab_out[k, s, pl.ds(t, nb, stride=pitch), :] = hcur
            h_scr[:, ls] = hcur
        hs_all = jnp.concatenate(
            [jnp.concatenate([slab_out[k, s, b * pitch:b * pitch + tt, :] for b in range(nb)], axis=0)
             for s in range(_NSLAB)], axis=1)
        og = (hs_all * gate_scr[r0:r0 + m_rows, :]).astype(BF16)
        m = _dot_tiles(og, w_out_ref, 0, D_MODEL // MXU_N)
        x = x_ref[:, k * tt:(k + 1) * tt, :].reshape(m_rows, D_MODEL)
        y_ref[:, k * tt:(k + 1) * tt, :] = (x + _rms(m, g_ref[1:2, :])).reshape(nb, tt, D_MODEL)

    @pl.when(i >= n_cast)
    def _():
        for k in range(n_sub):
            front(k)
        new_hist = xt_scr[n_sub * m_rows:n_sub * m_rows + nh, :]
        hist_out_ref[...] = new_hist
        xt_scr[0:nh, :] = new_hist
        for k in range(n_sub):
            back(k)
        h_out_ref[...] = h_scr[...]


def _rglru_layer(x, hist_tm, h0, gains, w_in, cw, cb, wax, bax, lam, w_out, *, tt, n_sub):
    nb, t_len, _ = x.shape
    t_step = tt * n_sub
    assert t_len % t_step == 0 and tt % SUBLANES == 0 and nb % SUBLANES == 0
    pitch = _seq_pitch(tt)
    m_rows = nb * t_step
    nh = _HIST * nb
    n_cast = _n_cast(w_in, w_out)
    n_tiles = t_len // t_step
    time_tile = lambda i: (0, _tile_index(i, n_cast, n_tiles), 0)
    w_in_spec, w_out_spec = _cast_specs(w_in, w_out)
    return pl.pallas_call(
        functools.partial(_rglru_kernel, n_cast=n_cast, nb=nb, tt=tt, n_sub=n_sub),
        grid=(n_cast + n_tiles,),
        in_specs=[
            pl.BlockSpec((nb, t_step, D_MODEL), time_tile),
            _const_spec(hist_tm.shape), _const_spec(h0.shape), _const_spec(gains.shape),
            w_in_spec, _const_spec(cw.shape), _const_spec(cb.shape),
            _const_spec(wax.shape), _const_spec(bax.shape), _const_spec(lam.shape),
            w_out_spec,
        ],
        out_specs=[
            pl.BlockSpec((nb, t_step, D_MODEL), time_tile),
            pl.BlockSpec((nh, D_RNN), lambda i: (0, 0)),
            pl.BlockSpec((nb, D_RNN), lambda i: (0, 0)),
        ],
        out_shape=[
            jax.ShapeDtypeStruct((nb, t_len, D_MODEL), F32),
            jax.ShapeDtypeStruct((nh, D_RNN), F32),
            jax.ShapeDtypeStruct((nb, D_RNN), F32),
        ],
        scratch_shapes=_cast_scratch(w_in, w_out) + [
            pltpu.VMEM((n_sub, _NSLAB, nb * pitch, LANES), F32),
            pltpu.VMEM((n_sub, _NSLAB, nb * pitch, LANES), F32),
            pltpu.VMEM((nh + m_rows, D_RNN), F32),
            pltpu.VMEM((m_rows, D_RNN), F32),
            pltpu.VMEM((m_rows, D_RNN), F32),
            pltpu.VMEM((m_rows, D_RNN), F32),
            pltpu.VMEM((nb, D_RNN), F32),
        ],
        compiler_params=_params(),
        name="rglru_mixer",
    )(x, hist_tm, h0, gains, w_in, cw, cb, wax, bax, lam, w_out)


def kernel(x_prompt, x_sample, state_conv, state_h, norm_gains, gm_w_in, gm_b_in, gm_ln_g, gm_ln_b,
           gm_w_s, gm_b_s, gm_w_out, rg_w_in, rg_conv_w, rg_conv_b, rg_w_a, rg_b_a, rg_w_x, rg_b_x,
           rg_lambda, rg_w_out, ffn_w_in, ffn_w_out):
    n_p, t_p, _ = x_prompt.shape
    n_s, t_s, _ = x_sample.shape
    assert CHUNK % t_s == 0 and t_p % CHUNK == 0

    gm_b_in2 = gm_b_in[0].reshape(1, -1)
    gm_ln_g2 = gm_ln_g[0].reshape(1, -1)
    gm_ln_b2 = gm_ln_b[0].reshape(1, -1)
    rep = CHUNK // t_s
    ws2 = jnp.stack([gm_w_s[0], jnp.tile(gm_w_s[0][:, :t_s, :t_s], (1, rep, rep))])
    bs2 = jnp.stack([gm_b_s[0].T, jnp.tile(gm_b_s[0][:, :t_s], (1, rep)).T])
    rg_wax = jnp.concatenate([rg_w_a[0], rg_w_x[0]], axis=-1).astype(BF16)
    rg_bax = jnp.concatenate([rg_b_a[0], rg_b_x[0]], axis=-1)
    rg_cb = rg_conv_b[0].reshape(1, -1)
    rg_lam = rg_lambda[0].reshape(1, -1)

    def ffn(xp, xs, li):
        return _ffn_layer(xp, xs, norm_gains, ffn_w_in, ffn_w_out, li, tm=512)

    def rglru(x, hist, h0, tt, n_sub):
        nb = x.shape[0]
        hist_tm = jnp.swapaxes(hist, 0, 1).reshape(_HIST * nb, D_RNN)
        y, new_hist_tm, h_last = _rglru_layer(x, hist_tm, h0, norm_gains[1], rg_w_in[0], rg_conv_w[0],
                                              rg_cb, rg_wax, rg_bax, rg_lam, rg_w_out[0],
                                              tt=tt, n_sub=n_sub)
        return y, jnp.swapaxes(new_hist_tm.reshape(_HIST, nb, D_RNN), 0, 1), h_last

    xp = x_prompt.reshape(n_p * t_p, D_MODEL)
    xs = x_sample.reshape(n_s * t_s, D_MODEL)
    xp, xs, s_v = _gmlp_layer(xp, xs, norm_gains[0], gm_w_in[0], gm_b_in2, gm_ln_g2, gm_ln_b2,
                              ws2, bs2, gm_w_out[0], tm=256, period_s=t_s)
    xp, xs = ffn(xp, xs, 0)
    xp, p_conv, p_h = rglru(xp.reshape(n_p, t_p, D_MODEL),
                            jnp.zeros((n_p, _HIST, D_RNN), F32), jnp.zeros((n_p, D_RNN), F32), 64, 1)
    xs, s_conv, s_h = rglru(xs.reshape(n_s, t_s, D_MODEL), state_conv[0], state_h[0], t_s, 1)
    y_prompt, y_sample = ffn(xp.reshape(n_p * t_p, D_MODEL), xs.reshape(n_s * t_s, D_MODEL), 1)

    return (y_prompt.reshape(n_p, t_p, D_MODEL), y_sample.reshape(n_s, t_s, D_MODEL),
            p_conv[None], p_h[None], s_conv[None], s_h[None], s_v.reshape(1, n_s, t_s, D_GMLP))
```

```python
import functools
import math

import jax
import jax.numpy as jnp
from jax import lax
from jax.experimental import pallas as pl
from jax.experimental.pallas import tpu as pltpu

D_MODEL = 1024
CHUNK = 128
D_GMLP = 3 * D_MODEL
G_A = 8
GW_A = D_GMLP // G_A
D_RNN = D_MODEL
H_B = 8
BW_B = D_RNN // H_B
CONV_W = 4
LRU_C = 8.0
EPS = 1e-6

LANES = 128
SUBLANES = 8
MXU_N = 256
VMEM_LIMIT_BYTES = 56 * 2 ** 20

BF16 = jnp.bfloat16
F32 = jnp.float32

_LOG2E = math.log2(math.e)
_GELU_C = math.sqrt(2.0 / math.pi)
_GELU_K1 = -2.0 * _GELU_C * _LOG2E
_GELU_K3 = -2.0 * _GELU_C * 0.044715 * _LOG2E


def _gelu(x):
    return x * (1.0 / (1.0 + jnp.exp2(x * (_GELU_K1 + _GELU_K3 * (x * x)))))


def _sigmoid(x):
    return 1.0 / (1.0 + jnp.exp2(x * (-_LOG2E)))


def _rms(x, g):
    ms = jnp.sum(x * x, axis=-1, keepdims=True) * (1.0 / x.shape[-1])
    return x * lax.rsqrt(ms + EPS) * g


def _dot(a, b):
    return jnp.dot(a, b, preferred_element_type=F32)


def _dot_tiles(a, w_ref, j0, n):
    parts = [_dot(a, w_ref[j]) for j in range(j0, j0 + n)]
    return parts[0] if n == 1 else jnp.concatenate(parts, axis=1)


def _const_spec(shape):
    return pl.BlockSpec(shape, lambda i: (0,) * len(shape), pipeline_mode=pl.Buffered(1))


_CAST_COLS = 2 * MXU_N


def _n_cast(w_in, w_out):
    n = w_in.shape[-1] // _CAST_COLS
    assert w_in.shape[-1] == n * _CAST_COLS and w_out.shape[-2] % (n * 2 * SUBLANES) == 0
    assert w_out.shape[-1] % MXU_N == 0
    return n


def _cast_specs(w_in, w_out, li=None):
    n = _n_cast(w_in, w_out)
    lead, pre = ((), ()) if li is None else ((None,), (li,))
    step = lambda i: jnp.minimum(i, n - 1)
    return [pl.BlockSpec(lead + (w_in.shape[-2], _CAST_COLS), lambda i: pre + (0, step(i))),
            pl.BlockSpec(lead + (w_out.shape[-2] // n, w_out.shape[-1]), lambda i: pre + (step(i), 0))]


def _cast_scratch(w_in, w_out):
    return [pltpu.VMEM((w_in.shape[-1] // MXU_N, w_in.shape[-2], MXU_N), BF16),
            pltpu.VMEM((w_out.shape[-1] // MXU_N, w_out.shape[-2], MXU_N), BF16)]


def _cast_step(i, w_in_ref, w_out_ref, w_in_scr, w_out_scr):
    blk = w_in_ref[...]
    for t in range(_CAST_COLS // MXU_N):
        w_in_scr[(_CAST_COLS // MXU_N) * i + t] = blk[:, t * MXU_N:(t + 1) * MXU_N].astype(BF16)
    rb = w_out_ref.shape[0]
    r0 = pl.multiple_of(i * rb, rb)
    blk = w_out_ref[...]
    for t in range(w_out_scr.shape[0]):
        w_out_scr[t, pl.ds(r0, rb), :] = blk[:, t * MXU_N:(t + 1) * MXU_N].astype(BF16)


def _tile_index(i, first, count):
    return jnp.clip(i - first, 0, count - 1)


def _params():
    return pltpu.CompilerParams(dimension_semantics=("arbitrary",),
                                vmem_limit_bytes=VMEM_LIMIT_BYTES)


_GM_NT = D_GMLP // MXU_N
_GM_UT = 3


def _gmlp_kernel(xp_ref, xs_ref, g_ref, w_in_f32, b_in_ref, ln_g_ref, ln_b_ref, ws_ref, bs_ref,
                 w_out_f32, yp_ref, ys_ref, v_ref, w_in_ref, w_out_ref, v_scr, vn_scr, p_scr,
                 *, n_cast, n_p, tm, period_s):
    i = pl.program_id(0)

    @pl.when(i < n_cast)
    def _():
        _cast_step(i, w_in_f32, w_out_f32, w_in_ref, w_out_ref)

    @pl.when(i >= n_cast)
    def _():
        _gmlp_tile(i < n_cast + n_p, xp_ref, xs_ref, g_ref, w_in_ref, b_in_ref, ln_g_ref, ln_b_ref,
                   ws_ref, bs_ref, w_out_ref, yp_ref, ys_ref, v_ref, v_scr, vn_scr, p_scr,
                   tm=tm, period_s=period_s)


def _gmlp_tile(is_prompt, xp_ref, xs_ref, g_ref, w_in_ref, b_in_ref, ln_g_ref, ln_b_ref, ws_ref,
               bs_ref, w_out_ref, yp_ref, ys_ref, v_ref, v_scr, vn_scr, p_scr, *, tm, period_s):
    is_sample = jnp.logical_not(is_prompt)
    x = jnp.where(is_prompt, xp_ref[...], xs_ref[...])
    hn = _rms(x, g_ref[0:1, :]).astype(BF16)

    s1 = jnp.zeros((tm, LANES), F32)
    s2 = jnp.zeros((tm, LANES), F32)
    for j in range(_GM_NT):
        c0 = D_GMLP + j * MXU_N
        z = _gelu(_dot(hn, w_in_ref[_GM_NT + j]) + b_in_ref[:, c0:c0 + MXU_N])
        v_scr[:, j * MXU_N:(j + 1) * MXU_N] = z
        zz = z * z
        s1 = s1 + (z[:, :LANES] + z[:, LANES:])
        s2 = s2 + (zz[:, :LANES] + zz[:, LANES:])
    mu = jnp.sum(s1, axis=-1, keepdims=True) * (1.0 / D_GMLP)
    var = jnp.sum(s2, axis=-1, keepdims=True) * (1.0 / D_GMLP) - mu * mu
    rstd = lax.rsqrt(var + EPS)
    vn = (v_scr[...] - mu) * rstd * ln_g_ref[...] + ln_b_ref[...]
    v_ref[...] = vn
    vn_scr[...] = vn.astype(BF16)

    row = lax.broadcasted_iota(jnp.int32, (CHUNK, CHUNK), 0)
    col = lax.broadcasted_iota(jnp.int32, (CHUNK, CHUNK), 1)
    first = jnp.where(is_prompt, 0, row - (row % period_s))
    keep = (col <= row) & (col >= first)

    gpc = _GM_UT * MXU_N // GW_A
    for j in range(_GM_NT // _GM_UT):
        c0 = j * _GM_UT * MXU_N
        u = _gelu(_dot_tiles(hn, w_in_ref, j * _GM_UT, _GM_UT) + b_in_ref[:, c0:c0 + _GM_UT * MXU_N])
        for gg in range(gpc):
            g = j * gpc + gg
            wsm = jnp.where(keep, ws_ref[g], 0.0).astype(BF16)
            bias = bs_ref[:, g:g + 1]
            for c in range(tm // CHUNK):
                r0 = c * CHUNK
                vg = vn_scr.at[r0:r0 + CHUNK, :]
                mixed = jnp.concatenate(
                    [_dot(wsm, vg[:, g * GW_A:g * GW_A + MXU_N]),
                     _dot(wsm, vg[:, g * GW_A + MXU_N:(g + 1) * GW_A])], axis=1) + bias
                p_scr[r0:r0 + CHUNK, g * GW_A:(g + 1) * GW_A] = (
                    u[r0:r0 + CHUNK, gg * GW_A:(gg + 1) * GW_A] * mixed).astype(BF16)

    m = _dot_tiles(p_scr[...], w_out_ref, 0, D_MODEL // MXU_N)
    y = x + _rms(m, g_ref[1:2, :])

    @pl.when(is_prompt)
    def _():
        yp_ref[...] = y

    @pl.when(is_sample)
    def _():
        ys_ref[...] = y


def _gmlp_layer(xp, xs, gains, w_in, b_in, ln_g, ln_b, ws2, bs2, w_out, *, tm, period_s):
    assert xp.shape[0] % tm == 0 and xs.shape[0] % tm == 0
    assert tm % CHUNK == 0 and CHUNK % period_s == 0
    n_cast = _n_cast(w_in, w_out)
    n_p, n_s = xp.shape[0] // tm, xs.shape[0] // tm
    p_tile = lambda i: (_tile_index(i, n_cast, n_p), 0)
    s_tile = lambda i: (_tile_index(i, n_cast + n_p, n_s), 0)
    which = lambda i: jnp.where(i < n_cast + n_p, 0, 1)
    w_in_spec, w_out_spec = _cast_specs(w_in, w_out)
    return pl.pallas_call(
        functools.partial(_gmlp_kernel, n_cast=n_cast, n_p=n_p, tm=tm, period_s=period_s),
        grid=(n_cast + n_p + n_s,),
        in_specs=[
            pl.BlockSpec((tm, D_MODEL), p_tile), pl.BlockSpec((tm, D_MODEL), s_tile),
            _const_spec(gains.shape), w_in_spec, _const_spec(b_in.shape),
            _const_spec(ln_g.shape), _const_spec(ln_b.shape),
            pl.BlockSpec((None,) + ws2.shape[1:], lambda i: (which(i), 0, 0, 0)),
            pl.BlockSpec((None,) + bs2.shape[1:], lambda i: (which(i), 0, 0)),
            w_out_spec,
        ],
        out_specs=[pl.BlockSpec((tm, D_MODEL), p_tile), pl.BlockSpec((tm, D_MODEL), s_tile),
                   pl.BlockSpec((tm, D_GMLP), s_tile)],
        out_shape=[jax.ShapeDtypeStruct(xp.shape, F32), jax.ShapeDtypeStruct(xs.shape, F32),
                   jax.ShapeDtypeStruct((xs.shape[0], D_GMLP), F32)],
        scratch_shapes=_cast_scratch(w_in, w_out) + [
            pltpu.VMEM((tm, D_GMLP), F32), pltpu.VMEM((tm, D_GMLP), BF16),
            pltpu.VMEM((tm, D_GMLP), BF16)],
        compiler_params=_params(),
        name="gmlp_mixer",
    )(xp, xs, gains, w_in, b_in, ln_g, ln_b, ws2, bs2, w_out)


def _ffn_kernel(xp_ref, xs_ref, g_ref, w_in_f32, w_out_f32, yp_ref, ys_ref,
                w_in_ref, w_out_ref, p_scr, *, n_cast, n_p):
    i = pl.program_id(0)

    @pl.when(i < n_cast)
    def _():
        _cast_step(i, w_in_f32, w_out_f32, w_in_ref, w_out_ref)

    @pl.when(i >= n_cast)
    def _():
        is_prompt = i < n_cast + n_p
        nt = w_in_ref.shape[0] // 2
        x = jnp.where(is_prompt, xp_ref[...], xs_ref[...])
        hn = _rms(x, g_ref[2:3, :]).astype(BF16)
        for j in range(nt):
            g = _dot(hn, w_in_ref[j])
            u = _dot(hn, w_in_ref[nt + j])
            p_scr[:, j * MXU_N:(j + 1) * MXU_N] = (g * _sigmoid(g) * u).astype(BF16)
        f = _dot_tiles(p_scr[...], w_out_ref, 0, D_MODEL // MXU_N)
        y = x + _rms(f, g_ref[3:4, :])

        @pl.when(is_prompt)
        def _():
            yp_ref[...] = y

        @pl.when(jnp.logical_not(is_prompt))
        def _():
            ys_ref[...] = y


def _ffn_layer(xp, xs, gains, w_in, w_out, li, *, tm):
    assert xp.shape[0] % tm == 0 and xs.shape[0] % tm == 0
    n_cast = _n_cast(w_in, w_out)
    n_p, n_s = xp.shape[0] // tm, xs.shape[0] // tm
    p_tile = lambda i: (_tile_index(i, n_cast, n_p), 0)
    s_tile = lambda i: (_tile_index(i, n_cast + n_p, n_s), 0)
    return pl.pallas_call(
        functools.partial(_ffn_kernel, n_cast=n_cast, n_p=n_p),
        grid=(n_cast + n_p + n_s,),
        in_specs=[pl.BlockSpec((tm, D_MODEL), p_tile), pl.BlockSpec((tm, D_MODEL), s_tile),
                  pl.BlockSpec((None,) + gains.shape[1:], lambda i: (li, 0, 0),
                               pipeline_mode=pl.Buffered(1))] + _cast_specs(w_in, w_out, li),
        out_specs=[pl.BlockSpec((tm, D_MODEL), p_tile), pl.BlockSpec((tm, D_MODEL), s_tile)],
        out_shape=[jax.ShapeDtypeStruct(xp.shape, F32), jax.ShapeDtypeStruct(xs.shape, F32)],
        scratch_shapes=_cast_scratch(w_in, w_out) + [pltpu.VMEM((tm, w_out.shape[-2]), BF16)],
        compiler_params=_params(),
        name="swiglu",
    )(xp, xs, gains, w_in, w_out)


_HIST = CONV_W - 1
_NSLAB = D_RNN // LANES


def _seq_pitch(tt):
    return tt if (tt // SUBLANES) % 2 == 1 else tt + SUBLANES


def _rglru_kernel(x_ref, hist_ref, h0_ref, g_ref, w_in_f32, cw_ref, cb_ref, wax_ref, bax_ref,
                  lam_ref, w_out_f32, y_ref, hist_out_ref, h_out_ref,
                  w_in_ref, w_out_ref, slab_in, slab_out, xt_scr, gate_scr, a_scr, b_scr, h_scr,
                  *, n_cast, nb, tt, n_sub):
    i = pl.program_id(0)
    m_rows = nb * tt
    pitch = _seq_pitch(tt)
    nh = _HIST * nb
    nt = D_RNN // MXU_N

    @pl.when(i == 0)
    def _():
        xt_scr[0:nh, :] = hist_ref[...]
        h_scr[...] = h0_ref[...]

    @pl.when(i < n_cast)
    def _():
        _cast_step(i, w_in_f32, w_out_f32, w_in_ref, w_out_ref)

    def front(k):
        rows = slice(k * m_rows, (k + 1) * m_rows)
        x = x_ref[:, k * tt:(k + 1) * tt, :].reshape(m_rows, D_MODEL)
        hn = _rms(x, g_ref[0:1, :]).astype(BF16)
        gate_scr[rows, :] = _gelu(_dot_tiles(hn, w_in_ref, 0, nt))
        xb = _dot_tiles(hn, w_in_ref, nt, nt)

        for s in range(_NSLAB):
            for b in range(nb):
                slab_in[k, s, b * pitch:b * pitch + tt, :] = (
                    xb[b * tt:(b + 1) * tt, s * LANES:(s + 1) * LANES])
        r0 = nh + k * m_rows
        for t in range(tt):
            for s in range(_NSLAB):
                xt_scr[r0 + t * nb:r0 + (t + 1) * nb, s * LANES:(s + 1) * LANES] = (
                    slab_in[k, s, pl.ds(t, nb, stride=pitch), :])

        xc = cb_ref[...]
        for kk in range(CONV_W):
            c0 = k * m_rows + kk * nb
            xc = xc + cw_ref[kk:kk + 1, :] * xt_scr[c0:c0 + m_rows, :]
        xcb = xc.astype(BF16)

        lam = lam_ref[...]
        c_sp = LRU_C * (jnp.maximum(-lam, 0.0) + jnp.log1p(jnp.exp(-jnp.abs(lam))))
        for h in range(H_B):
            hs = slice(h * BW_B, (h + 1) * BW_B)
            ri = _dot(xcb[:, hs], wax_ref[h]) + bax_ref[h:h + 1, :]
            r = _sigmoid(ri[:, :BW_B])
            ig = _sigmoid(ri[:, BW_B:])
            nla = c_sp[:, hs] * r
            a = jnp.exp2(nla * (-_LOG2E))
            mult = jnp.sqrt(jnp.tanh(nla) * (1.0 + a * a))
            a_scr[rows, hs] = a
            b_scr[rows, hs] = mult * (ig * xc[:, hs])

    def back(k):
        r0 = k * m_rows
        for s in range(_NSLAB):
            ls = slice(s * LANES, (s + 1) * LANES)
            hcur = h_scr[:, ls]
            for t in range(tt):
                tr = slice(r0 + t * nb, r0 + (t + 1) * nb)
                hcur = a_scr[tr, ls] * hcur + b_scr[tr, ls]
                slab_out[k, s, pl.ds(t, nb, stride=pitch), :] = hcur
            h_scr[:, ls] = hcur
        hs_all = jnp.concatenate(
            [jnp.concatenate([slab_out[k, s, b * pitch:b * pitch + tt, :] for b in range(nb)], axis=0)
             for s in range(_NSLAB)], axis=1)
        og = (hs_all * gate_scr[r0:r0 + m_rows, :]).astype(BF16)
        m = _dot_tiles(og, w_out_ref, 0, D_MODEL // MXU_N)
        x = x_ref[:, k * tt:(k + 1) * tt, :].reshape(m_rows, D_MODEL)
        y_ref[:, k * tt:(k + 1) * tt, :] = (x + _rms(m, g_ref[1:2, :])).reshape(nb, tt, D_MODEL)

    @pl.when(i >= n_cast)
    def _():
        for k in range(n_sub):
            front(k)
        new_hist = xt_scr[n_sub * m_rows:n_sub * m_rows + nh, :]
        hist_out_ref[...] = new_hist
        xt_scr[0:nh, :] = new_hist
        for k in range(n_sub):
            back(k)
        h_out_ref[...] = h_scr[...]


def _rglru_layer(x, hist_tm, h0, gains, w_in, cw, cb, wax, bax, lam, w_out, *, tt, n_sub):
    nb, t_len, _ = x.shape
    t_step = tt * n_sub
    assert t_len % t_step == 0 and tt % SUBLANES == 0 and nb % SUBLANES == 0
    pitch = _seq_pitch(tt)
    m_rows = nb * t_step
    nh = _HIST * nb
    n_cast = _n_cast(w_in, w_out)
    n_tiles = t_len // t_step
    time_tile = lambda i: (0, _tile_index(i, n_cast, n_tiles), 0)
    w_in_spec, w_out_spec = _cast_specs(w_in, w_out)
    return pl.pallas_call(
        functools.partial(_rglru_kernel, n_cast=n_cast, nb=nb, tt=tt, n_sub=n_sub),
        grid=(n_cast + n_tiles,),
        in_specs=[
            pl.BlockSpec((nb, t_step, D_MODEL), time_tile),
            _const_spec(hist_tm.shape), _const_spec(h0.shape), _const_spec(gains.shape),
            w_in_spec, _const_spec(cw.shape), _const_spec(cb.shape),
            _const_spec(wax.shape), _const_spec(bax.shape), _const_spec(lam.shape),
            w_out_spec,
        ],
        out_specs=[
            pl.BlockSpec((nb, t_step, D_MODEL), time_tile),
            pl.BlockSpec((nh, D_RNN), lambda i: (0, 0)),
            pl.BlockSpec((nb, D_RNN), lambda i: (0, 0)),
        ],
        out_shape=[
            jax.ShapeDtypeStruct((nb, t_len, D_MODEL), F32),
            jax.ShapeDtypeStruct((nh, D_RNN), F32),
            jax.ShapeDtypeStruct((nb, D_RNN), F32),
        ],
        scratch_shapes=_cast_scratch(w_in, w_out) + [
            pltpu.VMEM((n_sub, _NSLAB, nb * pitch, LANES), F32),
            pltpu.VMEM((n_sub, _NSLAB, nb * pitch, LANES), F32),
            pltpu.VMEM((nh + m_rows, D_RNN), F32),
            pltpu.VMEM((m_rows, D_RNN), F32),
            pltpu.VMEM((m_rows, D_RNN), F32),
            pltpu.VMEM((m_rows, D_RNN), F32),
            pltpu.VMEM((nb, D_RNN), F32),
        ],
        compiler_params=_params(),
        name="rglru_mixer",
    )(x, hist_tm, h0, gains, w_in, cw, cb, wax, bax, lam, w_out)


def kernel(x_prompt, x_sample, state_conv, state_h, norm_gains, gm_w_in, gm_b_in, gm_ln_g, gm_ln_b,
           gm_w_s, gm_b_s, gm_w_out, rg_w_in, rg_conv_w, rg_conv_b, rg_w_a, rg_b_a, rg_w_x, rg_b_x,
           rg_lambda, rg_w_out, ffn_w_in, ffn_w_out):
    n_p, t_p, _ = x_prompt.shape
    n_s, t_s, _ = x_sample.shape
    assert CHUNK % t_s == 0 and t_p % CHUNK == 0

    gm_b_in2 = gm_b_in[0].reshape(1, -1)
    gm_ln_g2 = gm_ln_g[0].reshape(1, -1)
    gm_ln_b2 = gm_ln_b[0].reshape(1, -1)
    rep = CHUNK // t_s
    ws2 = jnp.stack([gm_w_s[0], jnp.tile(gm_w_s[0][:, :t_s, :t_s], (1, rep, rep))])
    bs2 = jnp.stack([gm_b_s[0].T, jnp.tile(gm_b_s[0][:, :t_s], (1, rep)).T])
    rg_wax = jnp.concatenate([rg_w_a[0], rg_w_x[0]], axis=-1).astype(BF16)
    rg_bax = jnp.concatenate([rg_b_a[0], rg_b_x[0]], axis=-1)
    rg_cb = rg_conv_b[0].reshape(1, -1)
    rg_lam = rg_lambda[0].reshape(1, -1)

    def ffn(xp, xs, li):
        return _ffn_layer(xp, xs, norm_gains, ffn_w_in, ffn_w_out, li, tm=512)

    def rglru(x, hist, h0, tt, n_sub):
        nb = x.shape[0]
        hist_tm = jnp.swapaxes(hist, 0, 1).reshape(_HIST * nb, D_RNN)
        y, new_hist_tm, h_last = _rglru_layer(x, hist_tm, h0, norm_gains[1], rg_w_in[0], rg_conv_w[0],
                                              rg_cb, rg_wax, rg_bax, rg_lam, rg_w_out[0],
                                              tt=tt, n_sub=n_sub)
        return y, jnp.swapaxes(new_hist_tm.reshape(_HIST, nb, D_RNN), 0, 1), h_last

    xp = x_prompt.reshape(n_p * t_p, D_MODEL)
    xs = x_sample.reshape(n_s * t_s, D_MODEL)
    xp, xs, s_v = _gmlp_layer(xp, xs, norm_gains[0], gm_w_in[0], gm_b_in2, gm_ln_g2, gm_ln_b2,
                              ws2, bs2, gm_w_out[0], tm=256, period_s=t_s)
    xp, xs = ffn(xp, xs, 0)
    xp, p_conv, p_h = rglru(xp.reshape(n_p, t_p, D_MODEL),
                            jnp.zeros((n_p, _HIST, D_RNN), F32), jnp.zeros((n_p, D_RNN), F32), 64, 1)
    xs, s_conv, s_h = rglru(xs.reshape(n_s, t_s, D_MODEL), state_conv[0], state_h[0], t_s, 1)
    y_prompt, y_sample = ffn(xp.reshape(n_p * t_p, D_MODEL), xs.reshape(n_s * t_s, D_MODEL), 1)

    return (y_prompt.reshape(n_p, t_p, D_MODEL), y_sample.reshape(n_s, t_s, D_MODEL),
            p_conv[None], p_h[None], s_conv[None], s_h[None], s_v.reshape(1, n_s, t_s, D_GMLP))
```

```python
import functools
import math

import jax
import jax.numpy as jnp
from jax import lax
from jax.experimental import pallas as pl
from jax.experimental.pallas import tpu as pltpu

D_MODEL = 1024
CHUNK = 128
D_GMLP = 3 * D_MODEL
G_A = 8
GW_A = D_GMLP // G_A
D_RNN = D_MODEL
H_B = 8
BW_B = D_RNN // H_B
CONV_W = 4
LRU_C = 8.0
EPS = 1e-6

LANES = 128
SUBLANES = 8
MXU_N = 256
VMEM_LIMIT_BYTES = 56 * 2 ** 20

BF16 = jnp.bfloat16
F32 = jnp.float32

_LOG2E = math.log2(math.e)
_GELU_C = math.sqrt(2.0 / math.pi)
_GELU_K1 = -2.0 * _GELU_C * _LOG2E
_GELU_K3 = -2.0 * _GELU_C * 0.044715 * _LOG2E


def _gelu(x):
    return x * (1.0 / (1.0 + jnp.exp2(x * (_GELU_K1 + _GELU_K3 * (x * x)))))


def _sigmoid(x):
    return 1.0 / (1.0 + jnp.exp2(x * (-_LOG2E)))


def _rms(x, g):
    ms = jnp.sum(x * x, axis=-1, keepdims=True) * (1.0 / x.shape[-1])
    return x * lax.rsqrt(ms + EPS) * g


def _dot(a, b):
    return jnp.dot(a, b, preferred_element_type=F32)


def _dot_tiles(a, w_ref, j0, n):
    parts = [_dot(a, w_ref[j]) for j in range(j0, j0 + n)]
    return parts[0] if n == 1 else jnp.concatenate(parts, axis=1)


def _const_spec(shape):
    return pl.BlockSpec(shape, lambda i: (0,) * len(shape), pipeline_mode=pl.Buffered(1))


_CAST_COLS = 2 * MXU_N


def _n_cast(w_in, w_out):
    n = w_in.shape[-1] // _CAST_COLS
    assert w_in.shape[-1] == n * _CAST_COLS and w_out.shape[-2] % (n * 2 * SUBLANES) == 0
    assert w_out.shape[-1] % MXU_N == 0
    return n


def _cast_specs(w_in, w_out, li=None):
    n = _n_cast(w_in, w_out)
    lead, pre = ((), ()) if li is None else ((None,), (li,))
    step = lambda i: jnp.minimum(i, n - 1)
    return [pl.BlockSpec(lead + (w_in.shape[-2], _CAST_COLS), lambda i: pre + (0, step(i))),
            pl.BlockSpec(lead + (w_out.shape[-2] // n, w_out.shape[-1]), lambda i: pre + (step(i), 0))]


def _cast_scratch(w_in, w_out):
    return [pltpu.VMEM((w_in.shape[-1] // MXU_N, w_in.shape[-2], MXU_N), BF16),
            pltpu.VMEM((w_out.shape[-1] // MXU_N, w_out.shape[-2], MXU_N), BF16)]


def _cast_step(i, w_in_ref, w_out_ref, w_in_scr, w_out_scr):
    blk = w_in_ref[...]
    for t in range(_CAST_COLS // MXU_N):
        w_in_scr[(_CAST_COLS // MXU_N) * i + t] = blk[:, t * MXU_N:(t + 1) * MXU_N].astype(BF16)
    rb = w_out_ref.shape[0]
    r0 = pl.multiple_of(i * rb, rb)
    blk = w_out_ref[...]
    for t in range(w_out_scr.shape[0]):
        w_out_scr[t, pl.ds(r0, rb), :] = blk[:, t * MXU_N:(t + 1) * MXU_N].astype(BF16)


def _tile_index(i, first, count):
    return jnp.clip(i - first, 0, count - 1)


def _params():
    return pltpu.CompilerParams(dimension_semantics=("arbitrary",),
                                vmem_limit_bytes=VMEM_LIMIT_BYTES)


_GM_NT = D_GMLP // MXU_N
_GM_UT = 3


def _gmlp_kernel(xp_ref, xs_ref, g_ref, w_in_f32, b_in_ref, ln_g_ref, ln_b_ref, ws_ref, bs_ref,
                 w_out_f32, yp_ref, ys_ref, v_ref, w_in_ref, w_out_ref, v_scr, vn_scr, p_scr,
                 *, n_cast, n_p, tm, period_s):
    i = pl.program_id(0)

    @pl.when(i < n_cast)
    def _():
        _cast_step(i, w_in_f32, w_out_f32, w_in_ref, w_out_ref)

    @pl.when(i >= n_cast)
    def _():
        _gmlp_tile(i < n_cast + n_p, xp_ref, xs_ref, g_ref, w_in_ref, b_in_ref, ln_g_ref, ln_b_ref,
                   ws_ref, bs_ref, w_out_ref, yp_ref, ys_ref, v_ref, v_scr, vn_scr, p_scr,
                   tm=tm, period_s=period_s)


def _gmlp_tile(is_prompt, xp_ref, xs_ref, g_ref, w_in_ref, b_in_ref, ln_g_ref, ln_b_ref, ws_ref,
               bs_ref, w_out_ref, yp_ref, ys_ref, v_ref, v_scr, vn_scr, p_scr, *, tm, period_s):
    is_sample = jnp.logical_not(is_prompt)
    x = jnp.where(is_prompt, xp_ref[...], xs_ref[...])
    hn = _rms(x, g_ref[0:1, :]).astype(BF16)

    s1 = jnp.zeros((tm, LANES), F32)
    s2 = jnp.zeros((tm, LANES), F32)
    for j in range(_GM_NT):
        c0 = D_GMLP + j * MXU_N
        z = _gelu(_dot(hn, w_in_ref[_GM_NT + j]) + b_in_ref[:, c0:c0 + MXU_N])
        v_scr[:, j * MXU_N:(j + 1) * MXU_N] = z
        zz = z * z
        s1 = s1 + (z[:, :LANES] + z[:, LANES:])
        s2 = s2 + (zz[:, :LANES] + zz[:, LANES:])
    mu = jnp.sum(s1, axis=-1, keepdims=True) * (1.0 / D_GMLP)
    var = jnp.sum(s2, axis=-1, keepdims=True) * (1.0 / D_GMLP) - mu * mu
    rstd = lax.rsqrt(var + EPS)
    vn = (v_scr[...] - mu) * rstd * ln_g_ref[...] + ln_b_ref[...]
    v_ref[...] = vn
    vn_scr[...] = vn.astype(BF16)

    row = lax.broadcasted_iota(jnp.int32, (CHUNK, CHUNK), 0)
    col = lax.broadcasted_iota(jnp.int32, (CHUNK, CHUNK), 1)
    first = jnp.where(is_prompt, 0, row - (row % period_s))
    keep = (col <= row) & (col >= first)

    gpc = _GM_UT * MXU_N // GW_A
    for j in range(_GM_NT // _GM_UT):
        c0 = j * _GM_UT * MXU_N
        u = _gelu(_dot_tiles(hn, w_in_ref, j * _GM_UT, _GM_UT) + b_in_ref[:, c0:c0 + _GM_UT * MXU_N])
        for gg in range(gpc):
            g = j * gpc + gg
            wsm = jnp.where(keep, ws_ref[g], 0.0).astype(BF16)
            bias = bs_ref[:, g:g + 1]
            for c in range(tm // CHUNK):
                r0 = c * CHUNK
                vg = vn_scr.at[r0:r0 + CHUNK, :]
                mixed = jnp.concatenate(
                    [_dot(wsm, vg[:, g * GW_A:g * GW_A + MXU_N]),
                     _dot(wsm, vg[:, g * GW_A + MXU_N:(g + 1) * GW_A])], axis=1) + bias
                p_scr[r0:r0 + CHUNK, g * GW_A:(g + 1) * GW_A] = (
                    u[r0:r0 + CHUNK, gg * GW_A:(gg + 1) * GW_A] * mixed).astype(BF16)

    m = _dot_tiles(p_scr[...], w_out_ref, 0, D_MODEL // MXU_N)
    y = x + _rms(m, g_ref[1:2, :])

    @pl.when(is_prompt)
    def _():
        yp_ref[...] = y

    @pl.when(is_sample)
    def _():
        ys_ref[...] = y


def _gmlp_layer(xp, xs, gains, w_in, b_in, ln_g, ln_b, ws2, bs2, w_out, *, tm, period_s):
    assert xp.shape[0] % tm == 0 and xs.shape[0] % tm == 0
    assert tm % CHUNK == 0 and CHUNK % period_s == 0
    n_cast = _n_cast(w_in, w_out)
    n_p, n_s = xp.shape[0] // tm, xs.shape[0] // tm
    p_tile = lambda i: (_tile_index(i, n_cast, n_p), 0)
    s_tile = lambda i: (_tile_index(i, n_cast + n_p, n_s), 0)
    which = lambda i: jnp.where(i < n_cast + n_p, 0, 1)
    w_in_spec, w_out_spec = _cast_specs(w_in, w_out)
    return pl.pallas_call(
        functools.partial(_gmlp_kernel, n_cast=n_cast, n_p=n_p, tm=tm, period_s=period_s),
        grid=(n_cast + n_p + n_s,),
        in_specs=[
            pl.BlockSpec((tm, D_MODEL), p_tile), pl.BlockSpec((tm, D_MODEL), s_tile),
            _const_spec(gains.shape), w_in_spec, _const_spec(b_in.shape),
            _const_spec(ln_g.shape), _const_spec(ln_b.shape),
            pl.BlockSpec((None,) + ws2.shape[1:], lambda i: (which(i), 0, 0, 0)),
            pl.BlockSpec((None,) + bs2.shape[1:], lambda i: (which(i), 0, 0)),
            w_out_spec,
        ],
        out_specs=[pl.BlockSpec((tm, D_MODEL), p_tile), pl.BlockSpec((tm, D_MODEL), s_tile),
                   pl.BlockSpec((tm, D_GMLP), s_tile)],
        out_shape=[jax.ShapeDtypeStruct(xp.shape, F32), jax.ShapeDtypeStruct(xs.shape, F32),
                   jax.ShapeDtypeStruct((xs.shape[0], D_GMLP), F32)],
        scratch_shapes=_cast_scratch(w_in, w_out) + [
            pltpu.VMEM((tm, D_GMLP), F32), pltpu.VMEM((tm, D_GMLP), BF16),
            pltpu.VMEM((tm, D_GMLP), BF16)],
        compiler_params=_params(),
        name="gmlp_mixer",
    )(xp, xs, gains, w_in, b_in, ln_g, ln_b, ws2, bs2, w_out)


def _ffn_kernel(xp_ref, xs_ref, g_ref, w_in_f32, w_out_f32, yp_ref, ys_ref,
                w_in_ref, w_out_ref, p_scr, *, n_cast, n_p):
    i = pl.program_id(0)

    @pl.when(i < n_cast)
    def _():
        _cast_step(i, w_in_f32, w_out_f32, w_in_ref, w_out_ref)

    @pl.when(i >= n_cast)
    def _():
        is_prompt = i < n_cast + n_p
        nt = w_in_ref.shape[0] // 2
        x = jnp.where(is_prompt, xp_ref[...], xs_ref[...])
        half = x.shape[0] // 2
        xs_ = [x[k * half:(k + 1) * half, :] for k in range(2)]
        hns = [_rms(xk, g_ref[2:3, :]).astype(BF16) for xk in xs_]
        ys_ = []
        for k in range(2):
            rows = slice(k * half, (k + 1) * half)
            for j in range(nt):
                g = _dot(hns[k], w_in_ref[j])
                u = _dot(hns[k], w_in_ref[nt + j])
                p_scr[rows, j * MXU_N:(j + 1) * MXU_N] = (g * _sigmoid(g) * u).astype(BF16)
            f = _dot_tiles(p_scr[rows, :], w_out_ref, 0, D_MODEL // MXU_N)
            ys_.append(xs_[k] + _rms(f, g_ref[3:4, :]))
        y = jnp.concatenate(ys_, axis=0)

        @pl.when(is_prompt)
        def _():
            yp_ref[...] = y

        @pl.when(jnp.logical_not(is_prompt))
        def _():
            ys_ref[...] = y


def _ffn_layer(xp, xs, gains, w_in, w_out, li, *, tm):
    assert xp.shape[0] % tm == 0 and xs.shape[0] % tm == 0
    n_cast = _n_cast(w_in, w_out)
    n_p, n_s = xp.shape[0] // tm, xs.shape[0] // tm
    p_tile = lambda i: (_tile_index(i, n_cast, n_p), 0)
    s_tile = lambda i: (_tile_index(i, n_cast + n_p, n_s), 0)
    return pl.pallas_call(
        functools.partial(_ffn_kernel, n_cast=n_cast, n_p=n_p),
        grid=(n_cast + n_p + n_s,),
        in_specs=[pl.BlockSpec((tm, D_MODEL), p_tile), pl.BlockSpec((tm, D_MODEL), s_tile),
                  pl.BlockSpec((None,) + gains.shape[1:], lambda i: (li, 0, 0),
                               pipeline_mode=pl.Buffered(1))] + _cast_specs(w_in, w_out, li),
        out_specs=[pl.BlockSpec((tm, D_MODEL), p_tile), pl.BlockSpec((tm, D_MODEL), s_tile)],
        out_shape=[jax.ShapeDtypeStruct(xp.shape, F32), jax.ShapeDtypeStruct(xs.shape, F32)],
        scratch_shapes=_cast_scratch(w_in, w_out) + [pltpu.VMEM((tm, w_out.shape[-2]), BF16)],
        compiler_params=_params(),
        name="swiglu",
    )(xp, xs, gains, w_in, w_out)


_HIST = CONV_W - 1
_NSLAB = D_RNN // LANES


def _seq_pitch(tt):
    return tt if (tt // SUBLANES) % 2 == 1 else tt + SUBLANES


def _rglru_kernel(x_ref, hist_ref, h0_ref, g_ref, w_in_f32, cw_ref, cb_ref, wax_ref, bax_ref,
                  lam_ref, w_out_f32, y_ref, hist_out_ref, h_out_ref,
                  w_in_ref, w_out_ref, slab_in, slab_out, xt_scr, gate_scr, a_scr, b_scr, h_scr,
                  *, n_cast, nb, tt, n_sub):
    i = pl.program_id(0)
    m_rows = nb * tt
    pitch = _seq_pitch(tt)
    nh = _HIST * nb
    nt = D_RNN // MXU_N

    @pl.when(i == 0)
    def _():
        xt_scr[0:nh, :] = hist_ref[...]
        h_scr[...] = h0_ref[...]

    @pl.when(i < n_cast)
    def _():
        _cast_step(i, w_in_f32, w_out_f32, w_in_ref, w_out_ref)

    def front(k):
        rows = slice(k * m_rows, (k + 1) * m_rows)
        x = x_ref[:, k * tt:(k + 1) * tt, :].reshape(m_rows, D_MODEL)
        hn = _rms(x, g_ref[0:1, :]).astype(BF16)
        gate_scr[rows, :] = _gelu(_dot_tiles(hn, w_in_ref, 0, nt))
        xb = _dot_tiles(hn, w_in_ref, nt, nt)

        for s in range(_NSLAB):
            for b in range(nb):
                slab_in[k, s, b * pitch:b * pitch + tt, :] = (
                    xb[b * tt:(b + 1) * tt, s * LANES:(s + 1) * LANES])
        r0 = nh + k * m_rows
        for t in range(tt):
            for s in range(_NSLAB):
                xt_scr[r0 + t * nb:r0 + (t + 1) * nb, s * LANES:(s + 1) * LANES] = (
                    slab_in[k, s, pl.ds(t, nb, stride=pitch), :])

        xc = cb_ref[...]
        for kk in range(CONV_W):
            c0 = k * m_rows + kk * nb
            xc = xc + cw_ref[kk:kk + 1, :] * xt_scr[c0:c0 + m_rows, :]
        xcb = xc.astype(BF16)

        lam = lam_ref[...]
        c_sp = LRU_C * (jnp.maximum(-lam, 0.0) + jnp.log1p(jnp.exp(-jnp.abs(lam))))
        for h in range(H_B):
            hs = slice(h * BW_B, (h + 1) * BW_B)
            ri = _dot(xcb[:, hs], wax_ref[h]) + bax_ref[h:h + 1, :]
            r = _sigmoid(ri[:, :BW_B])
            ig = _sigmoid(ri[:, BW_B:])
            nla = c_sp[:, hs] * r
            a = jnp.exp2(nla * (-_LOG2E))
            mult = jnp.sqrt(jnp.tanh(nla) * (1.0 + a * a))
            a_scr[rows, hs] = a
            b_scr[rows, hs] = mult * (ig * xc[:, hs])

    def back(k):
        r0 = k * m_rows
        for s in range(_NSLAB):
            ls = slice(s * LANES, (s + 1) * LANES)
            hcur = h_scr[:, ls]
            for t in range(tt):
                tr = slice(r0 + t * nb, r0 + (t + 1) * nb)
                hcur = a_scr[tr, ls] * hcur + b_scr[tr, ls]
                slab_out[k, s, pl.ds(t, nb, stride=pitch), :] = hcur
            h_scr[:, ls] = hcur
        hs_all = jnp.concatenate(
            [jnp.concatenate([slab_out[k, s, b * pitch:b * pitch + tt, :] for b in range(nb)], axis=0)
             for s in range(_NSLAB)], axis=1)
        og = (hs_all * gate_scr[r0:r0 + m_rows, :]).astype(BF16)
        m = _dot_tiles(og, w_out_ref, 0, D_MODEL // MXU_N)
        x = x_ref[:, k * tt:(k + 1) * tt, :].reshape(m_rows, D_MODEL)
        y_ref[:, k * tt:(k + 1) * tt, :] = (x + _rms(m, g_ref[1:2, :])).reshape(nb, tt, D_MODEL)

    @pl.when(i >= n_cast)
    def _():
        for k in range(n_sub):
            front(k)
        new_hist = xt_scr[n_sub * m_rows:n_sub * m_rows + nh, :]
        hist_out_ref[...] = new_hist
        xt_scr[0:nh, :] = new_hist
        for k in range(n_sub):
            back(k)
        h_out_ref[...] = h_scr[...]


def _rglru_layer(x, hist_tm, h0, gains, w_in, cw, cb, wax, bax, lam, w_out, *, tt, n_sub):
    nb, t_len, _ = x.shape
    t_step = tt * n_sub
    assert t_len % t_step == 0 and tt % SUBLANES == 0 and nb % SUBLANES == 0
    pitch = _seq_pitch(tt)
    m_rows = nb * t_step
    nh = _HIST * nb
    n_cast = _n_cast(w_in, w_out)
    n_tiles = t_len // t_step
    time_tile = lambda i: (0, _tile_index(i, n_cast, n_tiles), 0)
    w_in_spec, w_out_spec = _cast_specs(w_in, w_out)
    return pl.pallas_call(
        functools.partial(_rglru_kernel, n_cast=n_cast, nb=nb, tt=tt, n_sub=n_sub),
        grid=(n_cast + n_tiles,),
        in_specs=[
            pl.BlockSpec((nb, t_step, D_MODEL), time_tile),
            _const_spec(hist_tm.shape), _const_spec(h0.shape), _const_spec(gains.shape),
            w_in_spec, _const_spec(cw.shape), _const_spec(cb.shape),
            _const_spec(wax.shape), _const_spec(bax.shape), _const_spec(lam.shape),
            w_out_spec,
        ],
        out_specs=[
            pl.BlockSpec((nb, t_step, D_MODEL), time_tile),
            pl.BlockSpec((nh, D_RNN), lambda i: (0, 0)),
            pl.BlockSpec((nb, D_RNN), lambda i: (0, 0)),
        ],
        out_shape=[
            jax.ShapeDtypeStruct((nb, t_len, D_MODEL), F32),
            jax.ShapeDtypeStruct((nh, D_RNN), F32),
            jax.ShapeDtypeStruct((nb, D_RNN), F32),
        ],
        scratch_shapes=_cast_scratch(w_in, w_out) + [
            pltpu.VMEM((n_sub, _NSLAB, nb * pitch, LANES), F32),
            pltpu.VMEM((n_sub, _NSLAB, nb * pitch, LANES), F32),
            pltpu.VMEM((nh + m_rows, D_RNN), F32),
            pltpu.VMEM((m_rows, D_RNN), F32),
            pltpu.VMEM((m_rows, D_RNN), F32),
            pltpu.VMEM((m_rows, D_RNN), F32),
            pltpu.VMEM((nb, D_RNN), F32),
        ],
        compiler_params=_params(),
        name="rglru_mixer",
    )(x, hist_tm, h0, gains, w_in, cw, cb, wax, bax, lam, w_out)


def kernel(x_prompt, x_sample, state_conv, state_h, norm_gains, gm_w_in, gm_b_in, gm_ln_g, gm_ln_b,
           gm_w_s, gm_b_s, gm_w_out, rg_w_in, rg_conv_w, rg_conv_b, rg_w_a, rg_b_a, rg_w_x, rg_b_x,
           rg_lambda, rg_w_out, ffn_w_in, ffn_w_out):
    n_p, t_p, _ = x_prompt.shape
    n_s, t_s, _ = x_sample.shape
    assert CHUNK % t_s == 0 and t_p % CHUNK == 0

    gm_b_in2 = gm_b_in[0].reshape(1, -1)
    gm_ln_g2 = gm_ln_g[0].reshape(1, -1)
    gm_ln_b2 = gm_ln_b[0].reshape(1, -1)
    rep = CHUNK // t_s
    ws2 = jnp.stack([gm_w_s[0], jnp.tile(gm_w_s[0][:, :t_s, :t_s], (1, rep, rep))])
    bs2 = jnp.stack([gm_b_s[0].T, jnp.tile(gm_b_s[0][:, :t_s], (1, rep)).T])
    rg_wax = jnp.concatenate([rg_w_a[0], rg_w_x[0]], axis=-1).astype(BF16)
    rg_bax = jnp.concatenate([rg_b_a[0], rg_b_x[0]], axis=-1)
    rg_cb = rg_conv_b[0].reshape(1, -1)
    rg_lam = rg_lambda[0].reshape(1, -1)

    def ffn(xp, xs, li):
        return _ffn_layer(xp, xs, norm_gains, ffn_w_in, ffn_w_out, li, tm=512)

    def rglru(x, hist, h0, tt, n_sub):
        nb = x.shape[0]
        hist_tm = jnp.swapaxes(hist, 0, 1).reshape(_HIST * nb, D_RNN)
        y, new_hist_tm, h_last = _rglru_layer(x, hist_tm, h0, norm_gains[1], rg_w_in[0], rg_conv_w[0],
                                              rg_cb, rg_wax, rg_bax, rg_lam, rg_w_out[0],
                                              tt=tt, n_sub=n_sub)
        return y, jnp.swapaxes(new_hist_tm.reshape(_HIST, nb, D_RNN), 0, 1), h_last

    xp = x_prompt.reshape(n_p * t_p, D_MODEL)
    xs = x_sample.reshape(n_s * t_s, D_MODEL)
    xp, xs, s_v = _gmlp_layer(xp, xs, norm_gains[0], gm_w_in[0], gm_b_in2, gm_ln_g2, gm_ln_b2,
                              ws2, bs2, gm_w_out[0], tm=256, period_s=t_s)
    xp, xs = ffn(xp, xs, 0)
    xp, p_conv, p_h = rglru(xp.reshape(n_p, t_p, D_MODEL),
                            jnp.zeros((n_p, _HIST, D_RNN), F32), jnp.zeros((n_p, D_RNN), F32), 64, 1)
    xs, s_conv, s_h = rglru(xs.reshape(n_s, t_s, D_MODEL), state_conv[0], state_h[0], t_s, 1)
    y_prompt, y_sample = ffn(xp.reshape(n_p * t_p, D_MODEL), xs.reshape(n_s * t_s, D_MODEL), 1)

    return (y_prompt.reshape(n_p, t_p, D_MODEL), y_sample.reshape(n_s, t_s, D_MODEL),
            p_conv[None], p_h[None], s_conv[None], s_h[None], s_v.reshape(1, n_s, t_s, D_GMLP))
```

```python
import functools
import math

import jax
import jax.numpy as jnp
from jax import lax
from jax.experimental import pallas as pl
from jax.experimental.pallas import tpu as pltpu

D_MODEL = 1024
CHUNK = 128
D_GMLP = 3 * D_MODEL
G_A = 8
GW_A = D_GMLP // G_A
D_RNN = D_MODEL
H_B = 8
BW_B = D_RNN // H_B
CONV_W = 4
LRU_C = 8.0
EPS = 1e-6

LANES = 128
SUBLANES = 8
MXU_N = 256
VMEM_LIMIT_BYTES = 56 * 2 ** 20

BF16 = jnp.bfloat16
F32 = jnp.float32

_LOG2E = math.log2(math.e)
_GELU_C = math.sqrt(2.0 / math.pi)
_GELU_K1 = -2.0 * _GELU_C * _LOG2E
_GELU_K3 = -2.0 * _GELU_C * 0.044715 * _LOG2E


def _gelu(x):
    return x * (1.0 / (1.0 + jnp.exp2(x * (_GELU_K1 + _GELU_K3 * (x * x)))))


def _sigmoid(x):
    return 1.0 / (1.0 + jnp.exp2(x * (-_LOG2E)))


def _rms(x, g):
    ms = jnp.sum(x * x, axis=-1, keepdims=True) * (1.0 / x.shape[-1])
    return x * lax.rsqrt(ms + EPS) * g


def _dot(a, b):
    return jnp.dot(a, b, preferred_element_type=F32)


def _dot_tiles(a, w_ref, j0, n):
    parts = [_dot(a, w_ref[j]) for j in range(j0, j0 + n)]
    return parts[0] if n == 1 else jnp.concatenate(parts, axis=1)


def _const_spec(shape):
    return pl.BlockSpec(shape, lambda i: (0,) * len(shape), pipeline_mode=pl.Buffered(1))


_CAST_COLS = 2 * MXU_N


def _n_cast(w_in, w_out):
    n = w_in.shape[-1] // _CAST_COLS
    assert w_in.shape[-1] == n * _CAST_COLS and w_out.shape[-2] % (n * 2 * SUBLANES) == 0
    assert w_out.shape[-1] % MXU_N == 0
    return n


def _cast_specs(w_in, w_out, li=None):
    n = _n_cast(w_in, w_out)
    lead, pre = ((), ()) if li is None else ((None,), (li,))
    step = lambda i: jnp.minimum(i, n - 1)
    return [pl.BlockSpec(lead + (w_in.shape[-2], _CAST_COLS), lambda i: pre + (0, step(i))),
            pl.BlockSpec(lead + (w_out.shape[-2] // n, w_out.shape[-1]), lambda i: pre + (step(i), 0))]


def _cast_scratch(w_in, w_out):
    return [pltpu.VMEM((w_in.shape[-1] // MXU_N, w_in.shape[-2], MXU_N), BF16),
            pltpu.VMEM((w_out.shape[-1] // MXU_N, w_out.shape[-2], MXU_N), BF16)]


def _cast_step(i, w_in_ref, w_out_ref, w_in_scr, w_out_scr):
    blk = w_in_ref[...]
    for t in range(_CAST_COLS // MXU_N):
        w_in_scr[(_CAST_COLS // MXU_N) * i + t] = blk[:, t * MXU_N:(t + 1) * MXU_N].astype(BF16)
    rb = w_out_ref.shape[0]
    r0 = pl.multiple_of(i * rb, rb)
    blk = w_out_ref[...]
    for t in range(w_out_scr.shape[0]):
        w_out_scr[t, pl.ds(r0, rb), :] = blk[:, t * MXU_N:(t + 1) * MXU_N].astype(BF16)


def _tile_index(i, first, count):
    return jnp.clip(i - first, 0, count - 1)


def _params():
    return pltpu.CompilerParams(dimension_semantics=("arbitrary",),
                                vmem_limit_bytes=VMEM_LIMIT_BYTES)


_GM_NT = D_GMLP // MXU_N
_GM_UT = 3


def _gmlp_kernel(xp_ref, xs_ref, g_ref, w_in_f32, b_in_ref, ln_g_ref, ln_b_ref, ws_p_ref,
                 ws_s_ref, bs_ref, w_out_f32, yp_ref, ys_ref, v_ref, w_in_ref, w_out_ref, v_scr, vn_scr, p_scr,
                 *, n_cast, n_p, tm, period_s):
    i = pl.program_id(0)

    @pl.when(i < n_cast)
    def _():
        _cast_step(i, w_in_f32, w_out_f32, w_in_ref, w_out_ref)

    @pl.when(i >= n_cast)
    def _():
        _gmlp_tile(i < n_cast + n_p, xp_ref, xs_ref, g_ref, w_in_ref, b_in_ref, ln_g_ref, ln_b_ref,
                   (ws_p_ref, ws_s_ref), bs_ref, w_out_ref, yp_ref, ys_ref, v_ref, v_scr, vn_scr,
                   p_scr,
                   tm=tm, period_s=period_s)


def _gmlp_tile(is_prompt, xp_ref, xs_ref, g_ref, w_in_ref, b_in_ref, ln_g_ref, ln_b_ref, ws_refs,
               bs_ref, w_out_ref, yp_ref, ys_ref, v_ref, v_scr, vn_scr, p_scr, *, tm, period_s):
    is_sample = jnp.logical_not(is_prompt)
    x = jnp.where(is_prompt, xp_ref[...], xs_ref[...])
    hn = _rms(x, g_ref[0:1, :]).astype(BF16)

    s1 = jnp.zeros((tm, LANES), F32)
    s2 = jnp.zeros((tm, LANES), F32)
    for j in range(_GM_NT):
        c0 = D_GMLP + j * MXU_N
        z = _gelu(_dot(hn, w_in_ref[_GM_NT + j]) + b_in_ref[:, c0:c0 + MXU_N])
        v_scr[:, j * MXU_N:(j + 1) * MXU_N] = z
        zz = z * z
        s1 = s1 + (z[:, :LANES] + z[:, LANES:])
        s2 = s2 + (zz[:, :LANES] + zz[:, LANES:])
    mu = jnp.sum(s1, axis=-1, keepdims=True) * (1.0 / D_GMLP)
    var = jnp.sum(s2, axis=-1, keepdims=True) * (1.0 / D_GMLP) - mu * mu
    rstd = lax.rsqrt(var + EPS)
    vn = (v_scr[...] - mu) * rstd * ln_g_ref[...] + ln_b_ref[...]
    v_ref[...] = vn
    vn_scr[...] = vn.astype(BF16)

    row = lax.broadcasted_iota(jnp.int32, (CHUNK, CHUNK), 0)
    col = lax.broadcasted_iota(jnp.int32, (CHUNK, CHUNK), 1)
    first = jnp.where(is_prompt, 0, row - (row % period_s))
    keep = (col <= row) & (col >= first)

    gpc = _GM_UT * MXU_N // GW_A
    for j in range(_GM_NT // _GM_UT):
        c0 = j * _GM_UT * MXU_N
        u = _gelu(_dot_tiles(hn, w_in_ref, j * _GM_UT, _GM_UT) + b_in_ref[:, c0:c0 + _GM_UT * MXU_N])
        for gg in range(gpc):
            g = j * gpc + gg
            ws_g = jnp.where(is_prompt, ws_refs[0][g], ws_refs[1][g])
            wsm = jnp.where(keep, ws_g, 0.0).astype(BF16)
            bias = bs_ref[:, g:g + 1]
            for c in range(tm // CHUNK):
                r0 = c * CHUNK
                vg = vn_scr.at[r0:r0 + CHUNK, :]
                mixed = jnp.concatenate(
                    [_dot(wsm, vg[:, g * GW_A:g * GW_A + MXU_N]),
                     _dot(wsm, vg[:, g * GW_A + MXU_N:(g + 1) * GW_A])], axis=1) + bias
                p_scr[r0:r0 + CHUNK, g * GW_A:(g + 1) * GW_A] = (
                    u[r0:r0 + CHUNK, gg * GW_A:(gg + 1) * GW_A] * mixed).astype(BF16)

    m = _dot_tiles(p_scr[...], w_out_ref, 0, D_MODEL // MXU_N)
    y = x + _rms(m, g_ref[1:2, :])

    @pl.when(is_prompt)
    def _():
        yp_ref[...] = y

    @pl.when(is_sample)
    def _():
        ys_ref[...] = y


def _gmlp_layer(xp, xs, gains, w_in, b_in, ln_g, ln_b, ws_p, ws_s, bs2, w_out, *, tm, period_s):
    assert xp.shape[0] % tm == 0 and xs.shape[0] % tm == 0
    assert tm % CHUNK == 0 and CHUNK % period_s == 0
    n_cast = _n_cast(w_in, w_out)
    n_p, n_s = xp.shape[0] // tm, xs.shape[0] // tm
    p_tile = lambda i: (_tile_index(i, n_cast, n_p), 0)
    s_tile = lambda i: (_tile_index(i, n_cast + n_p, n_s), 0)
    which = lambda i: jnp.where(i < n_cast + n_p, 0, 1)
    w_in_spec, w_out_spec = _cast_specs(w_in, w_out)
    return pl.pallas_call(
        functools.partial(_gmlp_kernel, n_cast=n_cast, n_p=n_p, tm=tm, period_s=period_s),
        grid=(n_cast + n_p + n_s,),
        in_specs=[
            pl.BlockSpec((tm, D_MODEL), p_tile), pl.BlockSpec((tm, D_MODEL), s_tile),
            _const_spec(gains.shape), w_in_spec, _const_spec(b_in.shape),
            _const_spec(ln_g.shape), _const_spec(ln_b.shape),
            _const_spec(ws_p.shape), _const_spec(ws_s.shape),
            pl.BlockSpec((None,) + bs2.shape[1:], lambda i: (which(i), 0, 0)),
            w_out_spec,
        ],
        out_specs=[pl.BlockSpec((tm, D_MODEL), p_tile), pl.BlockSpec((tm, D_MODEL), s_tile),
                   pl.BlockSpec((tm, D_GMLP), s_tile)],
        out_shape=[jax.ShapeDtypeStruct(xp.shape, F32), jax.ShapeDtypeStruct(xs.shape, F32),
                   jax.ShapeDtypeStruct((xs.shape[0], D_GMLP), F32)],
        scratch_shapes=_cast_scratch(w_in, w_out) + [
            pltpu.VMEM((tm, D_GMLP), F32), pltpu.VMEM((tm, D_GMLP), BF16),
            pltpu.VMEM((tm, D_GMLP), BF16)],
        compiler_params=_params(),
        name="gmlp_mixer",
    )(xp, xs, gains, w_in, b_in, ln_g, ln_b, ws_p, ws_s, bs2, w_out)


def _ffn_kernel(xp_ref, xs_ref, g_ref, w_in_f32, w_out_f32, yp_ref, ys_ref,
                w_in_ref, w_out_ref, p_scr, *, n_cast, n_p):
    i = pl.program_id(0)

    @pl.when(i < n_cast)
    def _():
        _cast_step(i, w_in_f32, w_out_f32, w_in_ref, w_out_ref)

    @pl.when(i >= n_cast)
    def _():
        is_prompt = i < n_cast + n_p
        nt = w_in_ref.shape[0] // 2
        x = jnp.where(is_prompt, xp_ref[...], xs_ref[...])
        half = x.shape[0] // 2
        xs_ = [x[k * half:(k + 1) * half, :] for k in range(2)]
        hns = [_rms(xk, g_ref[2:3, :]).astype(BF16) for xk in xs_]
        ys_ = []
        for k in range(2):
            rows = slice(k * half, (k + 1) * half)
            for j in range(nt):
                g = _dot(hns[k], w_in_ref[j])
                u = _dot(hns[k], w_in_ref[nt + j])
                p_scr[rows, j * MXU_N:(j + 1) * MXU_N] = (g * _sigmoid(g) * u).astype(BF16)
            f = _dot_tiles(p_scr[rows, :], w_out_ref, 0, D_MODEL // MXU_N)
            ys_.append(xs_[k] + _rms(f, g_ref[3:4, :]))
        y = jnp.concatenate(ys_, axis=0)

        @pl.when(is_prompt)
        def _():
            yp_ref[...] = y

        @pl.when(jnp.logical_not(is_prompt))
        def _():
            ys_ref[...] = y


def _ffn_layer(xp, xs, gains, w_in, w_out, li, *, tm):
    assert xp.shape[0] % tm == 0 and xs.shape[0] % tm == 0
    n_cast = _n_cast(w_in, w_out)
    n_p, n_s = xp.shape[0] // tm, xs.shape[0] // tm
    p_tile = lambda i: (_tile_index(i, n_cast, n_p), 0)
    s_tile = lambda i: (_tile_index(i, n_cast + n_p, n_s), 0)
    return pl.pallas_call(
        functools.partial(_ffn_kernel, n_cast=n_cast, n_p=n_p),
        grid=(n_cast + n_p + n_s,),
        in_specs=[pl.BlockSpec((tm, D_MODEL), p_tile), pl.BlockSpec((tm, D_MODEL), s_tile),
                  pl.BlockSpec((None,) + gains.shape[1:], lambda i: (li, 0, 0),
                               pipeline_mode=pl.Buffered(1))] + _cast_specs(w_in, w_out, li),
        out_specs=[pl.BlockSpec((tm, D_MODEL), p_tile), pl.BlockSpec((tm, D_MODEL), s_tile)],
        out_shape=[jax.ShapeDtypeStruct(xp.shape, F32), jax.ShapeDtypeStruct(xs.shape, F32)],
        scratch_shapes=_cast_scratch(w_in, w_out) + [pltpu.VMEM((tm, w_out.shape[-2]), BF16)],
        compiler_params=_params(),
        name="swiglu",
    )(xp, xs, gains, w_in, w_out)


_HIST = CONV_W - 1
_NSLAB = D_RNN // LANES


def _seq_pitch(tt):
    return tt if (tt // SUBLANES) % 2 == 1 else tt + SUBLANES


def _rglru_kernel(x_ref, *refs, n_cast, nb, tt, n_sub, fresh):
    if not fresh:
        hist_ref, h0_ref, *refs = refs
    (g_ref, w_in_f32, cw_ref, cb_ref, wax_ref, bax_ref, lam_ref, w_out_f32,
     y_ref, hist_out_ref, h_out_ref,
     w_in_ref, w_out_ref, slab_in, slab_out, xt_scr, gate_scr, a_scr, b_scr, h_scr) = refs
    i = pl.program_id(0)
    m_rows = nb * tt
    pitch = _seq_pitch(tt)
    nh = _HIST * nb
    nt = D_RNN // MXU_N

    @pl.when(i == 0)
    def _():
        xt_scr[0:nh, :] = jnp.zeros((nh, D_RNN), F32) if fresh else hist_ref[...]
        h_scr[...] = jnp.zeros((nb, D_RNN), F32) if fresh else h0_ref[...]

    @pl.when(i < n_cast)
    def _():
        _cast_step(i, w_in_f32, w_out_f32, w_in_ref, w_out_ref)

    def front(k):
        rows = slice(k * m_rows, (k + 1) * m_rows)
        x = x_ref[:, k * tt:(k + 1) * tt, :].reshape(m_rows, D_MODEL)
        hn = _rms(x, g_ref[0:1, :]).astype(BF16)
        gate_scr[rows, :] = _gelu(_dot_tiles(hn, w_in_ref, 0, nt))
        xb = _dot_tiles(hn, w_in_ref, nt, nt)

        for s in range(_NSLAB):
            for b in range(nb):
                slab_in[k, s, b * pitch:b * pitch + tt, :] = (
                    xb[b * tt:(b + 1) * tt, s * LANES:(s + 1) * LANES])
        r0 = nh + k * m_rows
        for t in range(tt):
            for s in range(_NSLAB):
                xt_scr[r0 + t * nb:r0 + (t + 1) * nb, s * LANES:(s + 1) * LANES] = (
                    slab_in[k, s, pl.ds(t, nb, stride=pitch), :])

        xc = cb_ref[...]
        for kk in range(CONV_W):
            c0 = k * m_rows + kk * nb
            xc = xc + cw_ref[kk:kk + 1, :] * xt_scr[c0:c0 + m_rows, :]
        xcb = xc.astype(BF16)

        lam = lam_ref[...]
        c_sp = LRU_C * (jnp.maximum(-lam, 0.0) + jnp.log1p(jnp.exp(-jnp.abs(lam))))
        for h in range(H_B):
            hs = slice(h * BW_B, (h + 1) * BW_B)
            ri = _dot(xcb[:, hs], wax_ref[h]) + bax_ref[h:h + 1, :]
            r = _sigmoid(ri[:, :BW_B])
            ig = _sigmoid(ri[:, BW_B:])
            nla = c_sp[:, hs] * r
            a = jnp.exp2(nla * (-_LOG2E))
            mult = jnp.sqrt(jnp.tanh(nla) * (1.0 + a * a))
            a_scr[rows, hs] = a
            b_scr[rows, hs] = mult * (ig * xc[:, hs])

    def back(k):
        r0 = k * m_rows
        for s in range(_NSLAB):
            ls = slice(s * LANES, (s + 1) * LANES)
            hcur = h_scr[:, ls]
            for t in range(tt):
                tr = slice(r0 + t * nb, r0 + (t + 1) * nb)
                hcur = a_scr[tr, ls] * hcur + b_scr[tr, ls]
                slab_out[k, s, pl.ds(t, nb, stride=pitch), :] = hcur
            h_scr[:, ls] = hcur
        hs_all = jnp.concatenate(
            [jnp.concatenate([slab_out[k, s, b * pitch:b * pitch + tt, :] for b in range(nb)], axis=0)
             for s in range(_NSLAB)], axis=1)
        og = (hs_all * gate_scr[r0:r0 + m_rows, :]).astype(BF16)
        m = _dot_tiles(og, w_out_ref, 0, D_MODEL // MXU_N)
        x = x_ref[:, k * tt:(k + 1) * tt, :].reshape(m_rows, D_MODEL)
        y_ref[:, k * tt:(k + 1) * tt, :] = (x + _rms(m, g_ref[1:2, :])).reshape(nb, tt, D_MODEL)

    @pl.when(i >= n_cast)
    def _():
        for k in range(n_sub):
            front(k)
        new_hist = xt_scr[n_sub * m_rows:n_sub * m_rows + nh, :]
        hist_out_ref[...] = new_hist
        xt_scr[0:nh, :] = new_hist
        for k in range(n_sub):
            back(k)
        h_out_ref[...] = h_scr[...]


def _rglru_layer(x, hist_tm, h0, gains, w_in, cw, cb, wax, bax, lam, w_out, *, tt, n_sub):
    nb, t_len, _ = x.shape
    t_step = tt * n_sub
    assert t_len % t_step == 0 and tt % SUBLANES == 0 and nb % SUBLANES == 0
    pitch = _seq_pitch(tt)
    m_rows = nb * t_step
    nh = _HIST * nb
    n_cast = _n_cast(w_in, w_out)
    n_tiles = t_len // t_step
    time_tile = lambda i: (0, _tile_index(i, n_cast, n_tiles), 0)
    w_in_spec, w_out_spec = _cast_specs(w_in, w_out)
    fresh = hist_tm is None
    assert fresh == (h0 is None)
    state = () if fresh else (hist_tm, h0)
    return pl.pallas_call(
        functools.partial(_rglru_kernel, n_cast=n_cast, nb=nb, tt=tt, n_sub=n_sub, fresh=fresh),
        grid=(n_cast + n_tiles,),
        in_specs=[pl.BlockSpec((nb, t_step, D_MODEL), time_tile)]
        + [_const_spec(s.shape) for s in state] + [
            _const_spec(gains.shape),
            w_in_spec, _const_spec(cw.shape), _const_spec(cb.shape),
            _const_spec(wax.shape), _const_spec(bax.shape), _const_spec(lam.shape),
            w_out_spec,
        ],
        out_specs=[
            pl.BlockSpec((nb, t_step, D_MODEL), time_tile),
            pl.BlockSpec((nh, D_RNN), lambda i: (0, 0)),
            pl.BlockSpec((nb, D_RNN), lambda i: (0, 0)),
        ],
        out_shape=[
            jax.ShapeDtypeStruct((nb, t_len, D_MODEL), F32),
            jax.ShapeDtypeStruct((nh, D_RNN), F32),
            jax.ShapeDtypeStruct((nb, D_RNN), F32),
        ],
        scratch_shapes=_cast_scratch(w_in, w_out) + [
            pltpu.VMEM((n_sub, _NSLAB, nb * pitch, LANES), F32),
            pltpu.VMEM((n_sub, _NSLAB, nb * pitch, LANES), F32),
            pltpu.VMEM((nh + m_rows, D_RNN), F32),
            pltpu.VMEM((m_rows, D_RNN), F32),
            pltpu.VMEM((m_rows, D_RNN), F32),
            pltpu.VMEM((m_rows, D_RNN), F32),
            pltpu.VMEM((nb, D_RNN), F32),
        ],
        compiler_params=_params(),
        name="rglru_mixer",
    )(x, *state, gains, w_in, cw, cb, wax, bax, lam, w_out)


def kernel(x_prompt, x_sample, state_conv, state_h, norm_gains, gm_w_in, gm_b_in, gm_ln_g, gm_ln_b,
           gm_w_s, gm_b_s, gm_w_out, rg_w_in, rg_conv_w, rg_conv_b, rg_w_a, rg_b_a, rg_w_x, rg_b_x,
           rg_lambda, rg_w_out, ffn_w_in, ffn_w_out):
    n_p, t_p, _ = x_prompt.shape
    n_s, t_s, _ = x_sample.shape
    assert CHUNK % t_s == 0 and t_p % CHUNK == 0

    gm_b_in2 = gm_b_in[0].reshape(1, -1)
    gm_ln_g2 = gm_ln_g[0].reshape(1, -1)
    gm_ln_b2 = gm_ln_b[0].reshape(1, -1)
    rep = CHUNK // t_s
    ws_s = jnp.tile(gm_w_s[0][:, :t_s, :t_s], (1, rep, rep))
    bs2 = jnp.stack([gm_b_s[0].T, jnp.tile(gm_b_s[0][:, :t_s], (1, rep)).T])
    rg_wax = jnp.concatenate([rg_w_a[0], rg_w_x[0]], axis=-1).astype(BF16)
    rg_bax = jnp.concatenate([rg_b_a[0], rg_b_x[0]], axis=-1)
    rg_cb = rg_conv_b[0].reshape(1, -1)
    rg_lam = rg_lambda[0].reshape(1, -1)

    def ffn(xp, xs, li):
        return _ffn_layer(xp, xs, norm_gains, ffn_w_in, ffn_w_out, li, tm=512)

    def rglru(x, hist, h0, tt, n_sub):
        nb = x.shape[0]
        hist_tm = None if hist is None else jnp.swapaxes(hist, 0, 1).reshape(_HIST * nb, D_RNN)
        y, new_hist_tm, h_last = _rglru_layer(x, hist_tm, h0, norm_gains[1], rg_w_in[0], rg_conv_w[0],
                                              rg_cb, rg_wax, rg_bax, rg_lam, rg_w_out[0],
                                              tt=tt, n_sub=n_sub)
        return y, jnp.swapaxes(new_hist_tm.reshape(_HIST, nb, D_RNN), 0, 1), h_last

    xp = x_prompt.reshape(n_p * t_p, D_MODEL)
    xs = x_sample.reshape(n_s * t_s, D_MODEL)
    xp, xs, s_v = _gmlp_layer(xp, xs, norm_gains[0], gm_w_in[0], gm_b_in2, gm_ln_g2, gm_ln_b2,
                              gm_w_s[0], ws_s, bs2, gm_w_out[0], tm=256, period_s=t_s)
    xp, xs = ffn(xp, xs, 0)
    xp, p_conv, p_h = rglru(xp.reshape(n_p, t_p, D_MODEL), None, None, 64, 1)
    xs, s_conv, s_h = rglru(xs.reshape(n_s, t_s, D_MODEL), state_conv[0], state_h[0], t_s, 1)
    y_prompt, y_sample = ffn(xp.reshape(n_p * t_p, D_MODEL), xs.reshape(n_s * t_s, D_MODEL), 1)

    return (y_prompt.reshape(n_p, t_p, D_MODEL), y_sample.reshape(n_s, t_s, D_MODEL),
            p_conv[None], p_h[None], s_conv[None], s_h[None], s_v.reshape(1, n_s, t_s, D_GMLP))
```

```python
import functools
import math

import jax
import jax.numpy as jnp
from jax import lax
from jax.experimental import pallas as pl
from jax.experimental.pallas import tpu as pltpu

D_MODEL = 1024
CHUNK = 128
D_GMLP = 3 * D_MODEL
G_A = 8
GW_A = D_GMLP // G_A
D_RNN = D_MODEL
H_B = 8
BW_B = D_RNN // H_B
CONV_W = 4
LRU_C = 8.0
EPS = 1e-6

LANES = 128
SUBLANES = 8
MXU_N = 256
VMEM_LIMIT_BYTES = 56 * 2 ** 20

BF16 = jnp.bfloat16
F32 = jnp.float32

_LOG2E = math.log2(math.e)
_GELU_C = math.sqrt(2.0 / math.pi)
_GELU_K1 = -2.0 * _GELU_C * _LOG2E
_GELU_K3 = -2.0 * _GELU_C * 0.044715 * _LOG2E


def _gelu(x):
    return x * (1.0 / (1.0 + jnp.exp2(x * (_GELU_K1 + _GELU_K3 * (x * x)))))


def _sigmoid(x):
    return 1.0 / (1.0 + jnp.exp2(x * (-_LOG2E)))


def _rms(x, g):
    ms = jnp.sum(x * x, axis=-1, keepdims=True) * (1.0 / x.shape[-1])
    return x * lax.rsqrt(ms + EPS) * g


def _dot(a, b):
    return jnp.dot(a, b, preferred_element_type=F32)


def _dot_tiles(a, w_ref, j0, n):
    parts = [_dot(a, w_ref[j]) for j in range(j0, j0 + n)]
    return parts[0] if n == 1 else jnp.concatenate(parts, axis=1)


def _const_spec(shape):
    return pl.BlockSpec(shape, lambda i: (0,) * len(shape), pipeline_mode=pl.Buffered(1))


_CAST_COLS = 2 * MXU_N


def _n_cast(w_in, w_out):
    n = w_in.shape[-1] // _CAST_COLS
    assert w_in.shape[-1] == n * _CAST_COLS and w_out.shape[-2] % (n * 2 * SUBLANES) == 0
    assert w_out.shape[-1] % MXU_N == 0
    return n


def _cast_specs(w_in, w_out, li=None):
    n = _n_cast(w_in, w_out)
    lead, pre = ((), ()) if li is None else ((None,), (li,))
    step = lambda i: jnp.minimum(i, n - 1)
    return [pl.BlockSpec(lead + (w_in.shape[-2], _CAST_COLS), lambda i: pre + (0, step(i))),
            pl.BlockSpec(lead + (w_out.shape[-2] // n, w_out.shape[-1]), lambda i: pre + (step(i), 0))]


def _cast_scratch(w_in, w_out):
    return [pltpu.VMEM((w_in.shape[-1] // MXU_N, w_in.shape[-2], MXU_N), BF16),
            pltpu.VMEM((w_out.shape[-1] // MXU_N, w_out.shape[-2], MXU_N), BF16)]


def _cast_step(i, w_in_ref, w_out_ref, w_in_scr, w_out_scr):
    blk = w_in_ref[...]
    for t in range(_CAST_COLS // MXU_N):
        w_in_scr[(_CAST_COLS // MXU_N) * i + t] = blk[:, t * MXU_N:(t + 1) * MXU_N].astype(BF16)
    rb = w_out_ref.shape[0]
    r0 = pl.multiple_of(i * rb, rb)
    blk = w_out_ref[...]
    for t in range(w_out_scr.shape[0]):
        w_out_scr[t, pl.ds(r0, rb), :] = blk[:, t * MXU_N:(t + 1) * MXU_N].astype(BF16)


def _tile_index(i, first, count):
    return jnp.clip(i - first, 0, count - 1)


def _params():
    return pltpu.CompilerParams(dimension_semantics=("arbitrary",),
                                vmem_limit_bytes=VMEM_LIMIT_BYTES)


_GM_NT = D_GMLP // MXU_N
_GM_UT = 3


def _gmlp_kernel(xp_ref, xs_ref, g_ref, w_in_f32, b_in_ref, ln_g_ref, ln_b_ref, ws_p_ref,
                 ws_s_ref, bs_ref, w_out_f32, yp_ref, ys_ref, v_ref, w_in_ref, w_out_ref, v_scr, vn_scr, p_scr,
                 *, n_cast, n_p, tm, period_s):
    i = pl.program_id(0)

    @pl.when(i < n_cast)
    def _():
        _cast_step(i, w_in_f32, w_out_f32, w_in_ref, w_out_ref)

    @pl.when(i >= n_cast)
    def _():
        _gmlp_tile(i < n_cast + n_p, xp_ref, xs_ref, g_ref, w_in_ref, b_in_ref, ln_g_ref, ln_b_ref,
                   (ws_p_ref, ws_s_ref), bs_ref, w_out_ref, yp_ref, ys_ref, v_ref, v_scr, vn_scr,
                   p_scr,
                   tm=tm, period_s=period_s)


def _gmlp_tile(is_prompt, xp_ref, xs_ref, g_ref, w_in_ref, b_in_ref, ln_g_ref, ln_b_ref, ws_refs,
               bs_ref, w_out_ref, yp_ref, ys_ref, v_ref, v_scr, vn_scr, p_scr, *, tm, period_s):
    is_sample = jnp.logical_not(is_prompt)
    x = jnp.where(is_prompt, xp_ref[...], xs_ref[...])
    hn = _rms(x, g_ref[0:1, :]).astype(BF16)

    s1 = jnp.zeros((tm, LANES), F32)
    s2 = jnp.zeros((tm, LANES), F32)
    for j in range(_GM_NT):
        c0 = D_GMLP + j * MXU_N
        z = _gelu(_dot(hn, w_in_ref[_GM_NT + j]) + b_in_ref[:, c0:c0 + MXU_N])
        v_scr[:, j * MXU_N:(j + 1) * MXU_N] = z
        zz = z * z
        s1 = s1 + (z[:, :LANES] + z[:, LANES:])
        s2 = s2 + (zz[:, :LANES] + zz[:, LANES:])
    mu = jnp.sum(s1, axis=-1, keepdims=True) * (1.0 / D_GMLP)
    var = jnp.sum(s2, axis=-1, keepdims=True) * (1.0 / D_GMLP) - mu * mu
    rstd = lax.rsqrt(var + EPS)
    vn = (v_scr[...] - mu) * rstd * ln_g_ref[...] + ln_b_ref[...]
    v_ref[...] = vn
    vn_scr[...] = vn.astype(BF16)

    row = lax.broadcasted_iota(jnp.int32, (CHUNK, CHUNK), 0)
    col = lax.broadcasted_iota(jnp.int32, (CHUNK, CHUNK), 1)
    first = jnp.where(is_prompt, 0, row - (row % period_s))
    keep = (col <= row) & (col >= first)

    gpc = _GM_UT * MXU_N // GW_A
    for j in range(_GM_NT // _GM_UT):
        c0 = j * _GM_UT * MXU_N
        u = _gelu(_dot_tiles(hn, w_in_ref, j * _GM_UT, _GM_UT) + b_in_ref[:, c0:c0 + _GM_UT * MXU_N])
        for gg in range(gpc):
            g = j * gpc + gg
            rep = CHUNK // period_s
            ws_s = jnp.concatenate([ws_refs[1][g]] * rep, axis=0)
            bs_s = jnp.concatenate([bs_ref[0:period_s, :]] * rep, axis=0)
            ws_g = jnp.where(is_prompt, ws_refs[0][g], ws_s)
            wsm = jnp.where(keep, ws_g, 0.0).astype(BF16)
            bias = jnp.where(is_prompt, bs_ref[...], bs_s)[:, g:g + 1]
            for c in range(tm // CHUNK):
                r0 = c * CHUNK
                vg = vn_scr.at[r0:r0 + CHUNK, :]
                mixed = jnp.concatenate(
                    [_dot(wsm, vg[:, g * GW_A:g * GW_A + MXU_N]),
                     _dot(wsm, vg[:, g * GW_A + MXU_N:(g + 1) * GW_A])], axis=1) + bias
                p_scr[r0:r0 + CHUNK, g * GW_A:(g + 1) * GW_A] = (
                    u[r0:r0 + CHUNK, gg * GW_A:(gg + 1) * GW_A] * mixed).astype(BF16)

    m = _dot_tiles(p_scr[...], w_out_ref, 0, D_MODEL // MXU_N)
    y = x + _rms(m, g_ref[1:2, :])

    @pl.when(is_prompt)
    def _():
        yp_ref[...] = y

    @pl.when(is_sample)
    def _():
        ys_ref[...] = y


def _gmlp_layer(xp, xs, gains, w_in, b_in, ln_g, ln_b, ws_p, ws_s, bs_t, w_out, *, tm, period_s):
    assert xp.shape[0] % tm == 0 and xs.shape[0] % tm == 0
    assert tm % CHUNK == 0 and CHUNK % period_s == 0 and period_s % SUBLANES == 0
    n_cast = _n_cast(w_in, w_out)
    n_p, n_s = xp.shape[0] // tm, xs.shape[0] // tm
    p_tile = lambda i: (_tile_index(i, n_cast, n_p), 0)
    s_tile = lambda i: (_tile_index(i, n_cast + n_p, n_s), 0)
    w_in_spec, w_out_spec = _cast_specs(w_in, w_out)
    return pl.pallas_call(
        functools.partial(_gmlp_kernel, n_cast=n_cast, n_p=n_p, tm=tm, period_s=period_s),
        grid=(n_cast + n_p + n_s,),
        in_specs=[
            pl.BlockSpec((tm, D_MODEL), p_tile), pl.BlockSpec((tm, D_MODEL), s_tile),
            _const_spec(gains.shape), w_in_spec, _const_spec(b_in.shape),
            _const_spec(ln_g.shape), _const_spec(ln_b.shape),
            _const_spec(ws_p.shape), _const_spec(ws_s.shape), _const_spec(bs_t.shape),
            w_out_spec,
        ],
        out_specs=[pl.BlockSpec((tm, D_MODEL), p_tile), pl.BlockSpec((tm, D_MODEL), s_tile),
                   pl.BlockSpec((tm, D_GMLP), s_tile)],
        out_shape=[jax.ShapeDtypeStruct(xp.shape, F32), jax.ShapeDtypeStruct(xs.shape, F32),
                   jax.ShapeDtypeStruct((xs.shape[0], D_GMLP), F32)],
        scratch_shapes=_cast_scratch(w_in, w_out) + [
            pltpu.VMEM((tm, D_GMLP), F32), pltpu.VMEM((tm, D_GMLP), BF16),
            pltpu.VMEM((tm, D_GMLP), BF16)],
        compiler_params=_params(),
        name="gmlp_mixer",
    )(xp, xs, gains, w_in, b_in, ln_g, ln_b, ws_p, ws_s, bs_t, w_out)


def _ffn_kernel(xp_ref, xs_ref, g_ref, w_in_f32, w_out_f32, yp_ref, ys_ref,
                w_in_ref, w_out_ref, p_scr, *, n_cast, n_p):
    i = pl.program_id(0)

    @pl.when(i < n_cast)
    def _():
        _cast_step(i, w_in_f32, w_out_f32, w_in_ref, w_out_ref)

    @pl.when(i >= n_cast)
    def _():
        is_prompt = i < n_cast + n_p
        nt = w_in_ref.shape[0] // 2
        x = jnp.where(is_prompt, xp_ref[...], xs_ref[...])
        half = x.shape[0] // 2
        xs_ = [x[k * half:(k + 1) * half, :] for k in range(2)]
        hns = [_rms(xk, g_ref[2:3, :]).astype(BF16) for xk in xs_]
        ys_ = []
        for k in range(2):
            rows = slice(k * half, (k + 1) * half)
            for j in range(nt):
                g = _dot(hns[k], w_in_ref[j])
                u = _dot(hns[k], w_in_ref[nt + j])
                p_scr[rows, j * MXU_N:(j + 1) * MXU_N] = (g * _sigmoid(g) * u).astype(BF16)
            f = _dot_tiles(p_scr[rows, :], w_out_ref, 0, D_MODEL // MXU_N)
            ys_.append(xs_[k] + _rms(f, g_ref[3:4, :]))
        y = jnp.concatenate(ys_, axis=0)

        @pl.when(is_prompt)
        def _():
            yp_ref[...] = y

        @pl.when(jnp.logical_not(is_prompt))
        def _():
            ys_ref[...] = y


def _ffn_layer(xp, xs, gains, w_in, w_out, li, *, tm):
    assert xp.shape[0] % tm == 0 and xs.shape[0] % tm == 0
    n_cast = _n_cast(w_in, w_out)
    n_p, n_s = xp.shape[0] // tm, xs.shape[0] // tm
    p_tile = lambda i: (_tile_index(i, n_cast, n_p), 0)
    s_tile = lambda i: (_tile_index(i, n_cast + n_p, n_s), 0)
    return pl.pallas_call(
        functools.partial(_ffn_kernel, n_cast=n_cast, n_p=n_p),
        grid=(n_cast + n_p + n_s,),
        in_specs=[pl.BlockSpec((tm, D_MODEL), p_tile), pl.BlockSpec((tm, D_MODEL), s_tile),
                  pl.BlockSpec((None,) + gains.shape[1:], lambda i: (li, 0, 0),
                               pipeline_mode=pl.Buffered(1))] + _cast_specs(w_in, w_out, li),
        out_specs=[pl.BlockSpec((tm, D_MODEL), p_tile), pl.BlockSpec((tm, D_MODEL), s_tile)],
        out_shape=[jax.ShapeDtypeStruct(xp.shape, F32), jax.ShapeDtypeStruct(xs.shape, F32)],
        scratch_shapes=_cast_scratch(w_in, w_out) + [pltpu.VMEM((tm, w_out.shape[-2]), BF16)],
        compiler_params=_params(),
        name="swiglu",
    )(xp, xs, gains, w_in, w_out)


_HIST = CONV_W - 1
_NSLAB = D_RNN // LANES


def _seq_pitch(tt):
    return tt if (tt // SUBLANES) % 2 == 1 else tt + SUBLANES


def _rglru_kernel(x_ref, *refs, n_cast, nb, tt, n_sub, fresh):
    if not fresh:
        hist_ref, h0_ref, *refs = refs
    (g_ref, w_in_f32, cw_ref, cb_ref, wax_ref, bax_ref, lam_ref, w_out_f32,
     y_ref, hist_out_ref, h_out_ref,
     w_in_ref, w_out_ref, slab_in, slab_out, xt_scr, gate_scr, a_scr, b_scr, h_scr) = refs
    i = pl.program_id(0)
    m_rows = nb * tt
    pitch = _seq_pitch(tt)
    nh = _HIST * nb
    nt = D_RNN // MXU_N

    @pl.when(i == 0)
    def _():
        xt_scr[0:nh, :] = jnp.zeros((nh, D_RNN), F32) if fresh else hist_ref[...]
        h_scr[...] = jnp.zeros((nb, D_RNN), F32) if fresh else h0_ref[...]

    @pl.when(i < n_cast)
    def _():
        _cast_step(i, w_in_f32, w_out_f32, w_in_ref, w_out_ref)

    def front(k):
        rows = slice(k * m_rows, (k + 1) * m_rows)
        x = x_ref[:, k * tt:(k + 1) * tt, :].reshape(m_rows, D_MODEL)
        hn = _rms(x, g_ref[0:1, :]).astype(BF16)
        gate_scr[rows, :] = _gelu(_dot_tiles(hn, w_in_ref, 0, nt))
        xb = _dot_tiles(hn, w_in_ref, nt, nt)

        for s in range(_NSLAB):
            for b in range(nb):
                slab_in[k, s, b * pitch:b * pitch + tt, :] = (
                    xb[b * tt:(b + 1) * tt, s * LANES:(s + 1) * LANES])
        r0 = nh + k * m_rows
        for t in range(tt):
            for s in range(_NSLAB):
                xt_scr[r0 + t * nb:r0 + (t + 1) * nb, s * LANES:(s + 1) * LANES] = (
                    slab_in[k, s, pl.ds(t, nb, stride=pitch), :])

        xc = cb_ref[...]
        for kk in range(CONV_W):
            c0 = k * m_rows + kk * nb
            xc = xc + cw_ref[kk:kk + 1, :] * xt_scr[c0:c0 + m_rows, :]
        xcb = xc.astype(BF16)

        lam = lam_ref[...]
        c_sp = LRU_C * (jnp.maximum(-lam, 0.0) + jnp.log1p(jnp.exp(-jnp.abs(lam))))
        for h in range(H_B):
            hs = slice(h * BW_B, (h + 1) * BW_B)
            ri = _dot(xcb[:, hs], wax_ref[h]) + bax_ref[h:h + 1, :]
            r = _sigmoid(ri[:, :BW_B])
            ig = _sigmoid(ri[:, BW_B:])
            nla = c_sp[:, hs] * r
            a = jnp.exp2(nla * (-_LOG2E))
            mult = jnp.sqrt(jnp.tanh(nla) * (1.0 + a * a))
            a_scr[rows, hs] = a
            b_scr[rows, hs] = mult * (ig * xc[:, hs])

    def back(k):
        r0 = k * m_rows
        for s in range(_NSLAB):
            ls = slice(s * LANES, (s + 1) * LANES)
            hcur = h_scr[:, ls]
            for t in range(tt):
                tr = slice(r0 + t * nb, r0 + (t + 1) * nb)
                hcur = a_scr[tr, ls] * hcur + b_scr[tr, ls]
                slab_out[k, s, pl.ds(t, nb, stride=pitch), :] = hcur
            h_scr[:, ls] = hcur
        hs_all = jnp.concatenate(
            [jnp.concatenate([slab_out[k, s, b * pitch:b * pitch + tt, :] for b in range(nb)], axis=0)
             for s in range(_NSLAB)], axis=1)
        og = (hs_all * gate_scr[r0:r0 + m_rows, :]).astype(BF16)
        m = _dot_tiles(og, w_out_ref, 0, D_MODEL // MXU_N)
        x = x_ref[:, k * tt:(k + 1) * tt, :].reshape(m_rows, D_MODEL)
        y_ref[:, k * tt:(k + 1) * tt, :] = (x + _rms(m, g_ref[1:2, :])).reshape(nb, tt, D_MODEL)

    @pl.when(i >= n_cast)
    def _():
        for k in range(n_sub):
            front(k)
        new_hist = xt_scr[n_sub * m_rows:n_sub * m_rows + nh, :]
        hist_out_ref[...] = new_hist
        xt_scr[0:nh, :] = new_hist
        for k in range(n_sub):
            back(k)
        h_out_ref[...] = h_scr[...]


def _rglru_layer(x, hist_tm, h0, gains, w_in, cw, cb, wax, bax, lam, w_out, *, tt, n_sub):
    nb, t_len, _ = x.shape
    t_step = tt * n_sub
    assert t_len % t_step == 0 and tt % SUBLANES == 0 and nb % SUBLANES == 0
    pitch = _seq_pitch(tt)
    m_rows = nb * t_step
    nh = _HIST * nb
    n_cast = _n_cast(w_in, w_out)
    n_tiles = t_len // t_step
    time_tile = lambda i: (0, _tile_index(i, n_cast, n_tiles), 0)
    w_in_spec, w_out_spec = _cast_specs(w_in, w_out)
    fresh = hist_tm is None
    assert fresh == (h0 is None)
    state = () if fresh else (hist_tm, h0)
    return pl.pallas_call(
        functools.partial(_rglru_kernel, n_cast=n_cast, nb=nb, tt=tt, n_sub=n_sub, fresh=fresh),
        grid=(n_cast + n_tiles,),
        in_specs=[pl.BlockSpec((nb, t_step, D_MODEL), time_tile)]
        + [_const_spec(s.shape) for s in state] + [
            _const_spec(gains.shape),
            w_in_spec, _const_spec(cw.shape), _const_spec(cb.shape),
            _const_spec(wax.shape), _const_spec(bax.shape), _const_spec(lam.shape),
            w_out_spec,
        ],
        out_specs=[
            pl.BlockSpec((nb, t_step, D_MODEL), time_tile),
            pl.BlockSpec((nh, D_RNN), lambda i: (0, 0)),
            pl.BlockSpec((nb, D_RNN), lambda i: (0, 0)),
        ],
        out_shape=[
            jax.ShapeDtypeStruct((nb, t_len, D_MODEL), F32),
            jax.ShapeDtypeStruct((nh, D_RNN), F32),
            jax.ShapeDtypeStruct((nb, D_RNN), F32),
        ],
        scratch_shapes=_cast_scratch(w_in, w_out) + [
            pltpu.VMEM((n_sub, _NSLAB, nb * pitch, LANES), F32),
            pltpu.VMEM((n_sub, _NSLAB, nb * pitch, LANES), F32),
            pltpu.VMEM((nh + m_rows, D_RNN), F32),
            pltpu.VMEM((m_rows, D_RNN), F32),
            pltpu.VMEM((m_rows, D_RNN), F32),
            pltpu.VMEM((m_rows, D_RNN), F32),
            pltpu.VMEM((nb, D_RNN), F32),
        ],
        compiler_params=_params(),
        name="rglru_mixer",
    )(x, *state, gains, w_in, cw, cb, wax, bax, lam, w_out)


def kernel(x_prompt, x_sample, state_conv, state_h, norm_gains, gm_w_in, gm_b_in, gm_ln_g, gm_ln_b,
           gm_w_s, gm_b_s, gm_w_out, rg_w_in, rg_conv_w, rg_conv_b, rg_w_a, rg_b_a, rg_w_x, rg_b_x,
           rg_lambda, rg_w_out, ffn_w_in, ffn_w_out):
    n_p, t_p, _ = x_prompt.shape
    n_s, t_s, _ = x_sample.shape
    assert CHUNK % t_s == 0 and t_p % CHUNK == 0

    gm_b_in2 = gm_b_in[0].reshape(1, -1)
    gm_ln_g2 = gm_ln_g[0].reshape(1, -1)
    gm_ln_b2 = gm_ln_b[0].reshape(1, -1)
    rep = CHUNK // t_s
    ws_s = jnp.tile(gm_w_s[0][:, :t_s, :t_s], (1, 1, rep))
    bs_t = gm_b_s[0].T
    rg_wax = jnp.concatenate([rg_w_a[0], rg_w_x[0]], axis=-1).astype(BF16)
    rg_bax = jnp.concatenate([rg_b_a[0], rg_b_x[0]], axis=-1)
    rg_cb = rg_conv_b[0].reshape(1, -1)
    rg_lam = rg_lambda[0].reshape(1, -1)

    def ffn(xp, xs, li):
        return _ffn_layer(xp, xs, norm_gains, ffn_w_in, ffn_w_out, li, tm=512)

    def rglru(x, hist, h0, tt, n_sub):
        nb = x.shape[0]
        hist_tm = None if hist is None else jnp.swapaxes(hist, 0, 1).reshape(_HIST * nb, D_RNN)
        y, new_hist_tm, h_last = _rglru_layer(x, hist_tm, h0, norm_gains[1], rg_w_in[0], rg_conv_w[0],
                                              rg_cb, rg_wax, rg_bax, rg_lam, rg_w_out[0],
                                              tt=tt, n_sub=n_sub)
        return y, jnp.swapaxes(new_hist_tm.reshape(_HIST, nb, D_RNN), 0, 1), h_last

    xp = x_prompt.reshape(n_p * t_p, D_MODEL)
    xs = x_sample.reshape(n_s * t_s, D_MODEL)
    xp, xs, s_v = _gmlp_layer(xp, xs, norm_gains[0], gm_w_in[0], gm_b_in2, gm_ln_g2, gm_ln_b2,
                              gm_w_s[0], ws_s, bs_t, gm_w_out[0], tm=256, period_s=t_s)
    xp, xs = ffn(xp, xs, 0)
    xp, p_conv, p_h = rglru(xp.reshape(n_p, t_p, D_MODEL), None, None, 64, 1)
    xs, s_conv, s_h = rglru(xs.reshape(n_s, t_s, D_MODEL), state_conv[0], state_h[0], t_s, 1)
    y_prompt, y_sample = ffn(xp.reshape(n_p * t_p, D_MODEL), xs.reshape(n_s * t_s, D_MODEL), 1)

    return (y_prompt.reshape(n_p, t_p, D_MODEL), y_sample.reshape(n_s, t_s, D_MODEL),
            p_conv[None], p_h[None], s_conv[None], s_h[None], s_v.reshape(1, n_s, t_s, D_GMLP))
```

```python
import functools
import math

import jax
import jax.numpy as jnp
from jax import lax
from jax.experimental import pallas as pl
from jax.experimental.pallas import tpu as pltpu

D_MODEL = 1024
CHUNK = 128
D_GMLP = 3 * D_MODEL
G_A = 8
GW_A = D_GMLP // G_A
D_RNN = D_MODEL
H_B = 8
BW_B = D_RNN // H_B
CONV_W = 4
LRU_C = 8.0
EPS = 1e-6

LANES = 128
SUBLANES = 8
MXU_N = 256
VMEM_LIMIT_BYTES = 56 * 2 ** 20

BF16 = jnp.bfloat16
F32 = jnp.float32

_LOG2E = math.log2(math.e)
_GELU_C = math.sqrt(2.0 / math.pi)
_GELU_K1 = -2.0 * _GELU_C * _LOG2E
_GELU_K3 = -2.0 * _GELU_C * 0.044715 * _LOG2E


def _gelu(x):
    return x * (1.0 / (1.0 + jnp.exp2(x * (_GELU_K1 + _GELU_K3 * (x * x)))))


def _sigmoid(x):
    return 1.0 / (1.0 + jnp.exp2(x * (-_LOG2E)))


def _rms(x, g):
    ms = jnp.sum(x * x, axis=-1, keepdims=True) * (1.0 / x.shape[-1])
    return x * lax.rsqrt(ms + EPS) * g


def _dot(a, b):
    return jnp.dot(a, b, preferred_element_type=F32)


def _dot_tiles(a, w_ref, j0, n):
    parts = [_dot(a, w_ref[j]) for j in range(j0, j0 + n)]
    return parts[0] if n == 1 else jnp.concatenate(parts, axis=1)


def _const_spec(shape):
    return pl.BlockSpec(shape, lambda i: (0,) * len(shape), pipeline_mode=pl.Buffered(1))


_CAST_COLS = 2 * MXU_N


def _n_cast(w_in, w_out):
    n = w_in.shape[-1] // _CAST_COLS
    assert w_in.shape[-1] == n * _CAST_COLS and w_out.shape[-2] % (n * 2 * SUBLANES) == 0
    assert w_out.shape[-1] % MXU_N == 0
    return n


def _cast_specs(w_in, w_out, li=None):
    n = _n_cast(w_in, w_out)
    lead, pre = ((), ()) if li is None else ((None,), (li,))
    step = lambda i: jnp.minimum(i, n - 1)
    return [pl.BlockSpec(lead + (w_in.shape[-2], _CAST_COLS), lambda i: pre + (0, step(i))),
            pl.BlockSpec(lead + (w_out.shape[-2] // n, w_out.shape[-1]), lambda i: pre + (step(i), 0))]


def _cast_scratch(w_in, w_out):
    return [pltpu.VMEM((w_in.shape[-1] // MXU_N, w_in.shape[-2], MXU_N), BF16),
            pltpu.VMEM((w_out.shape[-1] // MXU_N, w_out.shape[-2], MXU_N), BF16)]


def _cast_step(i, w_in_ref, w_out_ref, w_in_scr, w_out_scr):
    blk = w_in_ref[...]
    for t in range(_CAST_COLS // MXU_N):
        w_in_scr[(_CAST_COLS // MXU_N) * i + t] = blk[:, t * MXU_N:(t + 1) * MXU_N].astype(BF16)
    rb = w_out_ref.shape[0]
    r0 = pl.multiple_of(i * rb, rb)
    blk = w_out_ref[...]
    for t in range(w_out_scr.shape[0]):
        w_out_scr[t, pl.ds(r0, rb), :] = blk[:, t * MXU_N:(t + 1) * MXU_N].astype(BF16)


def _tile_index(i, first, count):
    return jnp.clip(i - first, 0, count - 1)


def _params():
    return pltpu.CompilerParams(dimension_semantics=("arbitrary",),
                                vmem_limit_bytes=VMEM_LIMIT_BYTES)


_GM_NT = D_GMLP // MXU_N
_GM_UT = 3


def _gmlp_kernel(xp_ref, xs_ref, g_ref, w_in_f32, b_in_ref, ln_g_ref, ln_b_ref, ws_p_ref,
                 ws_s_ref, bs_ref, w_out_f32, yp_ref, ys_ref, v_ref, w_in_ref, w_out_ref, v_scr, vn_scr, p_scr,
                 *, n_cast, n_p, tm, period_s):
    i = pl.program_id(0)

    @pl.when(i < n_cast)
    def _():
        _cast_step(i, w_in_f32, w_out_f32, w_in_ref, w_out_ref)

    @pl.when(i >= n_cast)
    def _():
        _gmlp_tile(i < n_cast + n_p, xp_ref, xs_ref, g_ref, w_in_ref, b_in_ref, ln_g_ref, ln_b_ref,
                   (ws_p_ref, ws_s_ref), bs_ref, w_out_ref, yp_ref, ys_ref, v_ref, v_scr, vn_scr,
                   p_scr,
                   tm=tm, period_s=period_s)


def _gmlp_tile(is_prompt, xp_ref, xs_ref, g_ref, w_in_ref, b_in_ref, ln_g_ref, ln_b_ref, ws_refs,
               bs_ref, w_out_ref, yp_ref, ys_ref, v_ref, v_scr, vn_scr, p_scr, *, tm, period_s):
    is_sample = jnp.logical_not(is_prompt)
    x = jnp.where(is_prompt, xp_ref[...], xs_ref[...])
    hn = _rms(x, g_ref[0:1, :]).astype(BF16)

    s1 = jnp.zeros((tm, LANES), F32)
    s2 = jnp.zeros((tm, LANES), F32)
    for j in range(_GM_NT):
        c0 = D_GMLP + j * MXU_N
        z = _gelu(_dot(hn, w_in_ref[_GM_NT + j]) + b_in_ref[:, c0:c0 + MXU_N])
        v_scr[:, j * MXU_N:(j + 1) * MXU_N] = z
        zz = z * z
        s1 = s1 + (z[:, :LANES] + z[:, LANES:])
        s2 = s2 + (zz[:, :LANES] + zz[:, LANES:])
    mu = jnp.sum(s1, axis=-1, keepdims=True) * (1.0 / D_GMLP)
    var = jnp.sum(s2, axis=-1, keepdims=True) * (1.0 / D_GMLP) - mu * mu
    rstd = lax.rsqrt(var + EPS)
    vn = (v_scr[...] - mu) * rstd * ln_g_ref[...] + ln_b_ref[...]
    v_ref[...] = vn
    vn_scr[...] = vn.astype(BF16)

    row = lax.broadcasted_iota(jnp.int32, (CHUNK, CHUNK), 0)
    col = lax.broadcasted_iota(jnp.int32, (CHUNK, CHUNK), 1)
    first = jnp.where(is_prompt, 0, row - (row % period_s))
    keep = (col <= row) & (col >= first)

    gpc = _GM_UT * MXU_N // GW_A
    for j in range(_GM_NT // _GM_UT):
        c0 = j * _GM_UT * MXU_N
        u = _gelu(_dot_tiles(hn, w_in_ref, j * _GM_UT, _GM_UT) + b_in_ref[:, c0:c0 + _GM_UT * MXU_N])
        for gg in range(gpc):
            g = j * gpc + gg
            rep = CHUNK // period_s
            ws_s = jnp.concatenate([ws_refs[1][g]] * rep, axis=0)
            bs_s = jnp.concatenate([bs_ref[0:period_s, :]] * rep, axis=0)
            ws_g = jnp.where(is_prompt, ws_refs[0][g], ws_s)
            wsm = jnp.where(keep, ws_g, 0.0).astype(BF16)
            bias = jnp.where(is_prompt, bs_ref[...], bs_s)[:, g:g + 1]
            for c in range(tm // CHUNK):
                r0 = c * CHUNK
                vg = vn_scr.at[r0:r0 + CHUNK, :]
                mixed = jnp.concatenate(
                    [_dot(wsm, vg[:, g * GW_A:g * GW_A + MXU_N]),
                     _dot(wsm, vg[:, g * GW_A + MXU_N:(g + 1) * GW_A])], axis=1) + bias
                p_scr[r0:r0 + CHUNK, g * GW_A:(g + 1) * GW_A] = (
                    u[r0:r0 + CHUNK, gg * GW_A:(gg + 1) * GW_A] * mixed).astype(BF16)

    m = _dot_tiles(p_scr[...], w_out_ref, 0, D_MODEL // MXU_N)
    y = x + _rms(m, g_ref[1:2, :])

    @pl.when(is_prompt)
    def _():
        yp_ref[...] = y

    @pl.when(is_sample)
    def _():
        ys_ref[...] = y


def _gmlp_layer(xp, xs, gains, w_in, b_in, ln_g, ln_b, ws_p, ws_s, bs_t, w_out, *, tm, period_s):
    assert xp.shape[0] % tm == 0 and xs.shape[0] % tm == 0
    assert tm % CHUNK == 0 and CHUNK % period_s == 0 and period_s % SUBLANES == 0
    n_cast = _n_cast(w_in, w_out)
    n_p, n_s = xp.shape[0] // tm, xs.shape[0] // tm
    p_tile = lambda i: (_tile_index(i, n_cast, n_p), 0)
    s_tile = lambda i: (_tile_index(i, n_cast + n_p, n_s), 0)
    w_in_spec, w_out_spec = _cast_specs(w_in, w_out)
    return pl.pallas_call(
        functools.partial(_gmlp_kernel, n_cast=n_cast, n_p=n_p, tm=tm, period_s=period_s),
        grid=(n_cast + n_p + n_s,),
        in_specs=[
            pl.BlockSpec((tm, D_MODEL), p_tile), pl.BlockSpec((tm, D_MODEL), s_tile),
            _const_spec(gains.shape), w_in_spec, _const_spec(b_in.shape),
            _const_spec(ln_g.shape), _const_spec(ln_b.shape),
            _const_spec(ws_p.shape), _const_spec(ws_s.shape), _const_spec(bs_t.shape),
            w_out_spec,
        ],
        out_specs=[pl.BlockSpec((tm, D_MODEL), p_tile), pl.BlockSpec((tm, D_MODEL), s_tile),
                   pl.BlockSpec((tm, D_GMLP), s_tile)],
        out_shape=[jax.ShapeDtypeStruct(xp.shape, F32), jax.ShapeDtypeStruct(xs.shape, F32),
                   jax.ShapeDtypeStruct((xs.shape[0], D_GMLP), F32)],
        scratch_shapes=_cast_scratch(w_in, w_out) + [
            pltpu.VMEM((tm, D_GMLP), F32), pltpu.VMEM((tm, D_GMLP), BF16),
            pltpu.VMEM((tm, D_GMLP), BF16)],
        compiler_params=_params(),
        name="gmlp_mixer",
    )(xp, xs, gains, w_in, b_in, ln_g, ln_b, ws_p, ws_s, bs_t, w_out)


def _ffn_kernel(xp_ref, xs_ref, g_ref, w_in_f32, w_out_f32, yp_ref, ys_ref,
                w_in_ref, w_out_ref, p_scr, *, n_cast, n_p):
    i = pl.program_id(0)

    @pl.when(i < n_cast)
    def _():
        _cast_step(i, w_in_f32, w_out_f32, w_in_ref, w_out_ref)

    @pl.when(i >= n_cast)
    def _():
        is_prompt = i < n_cast + n_p
        nt = w_in_ref.shape[0] // 2
        x = jnp.where(is_prompt, xp_ref[...], xs_ref[...])
        half = x.shape[0] // 2
        xs_ = [x[k * half:(k + 1) * half, :] for k in range(2)]
        hns = [_rms(xk, g_ref[2:3, :]).astype(BF16) for xk in xs_]
        ys_ = []
        for k in range(2):
            rows = slice(k * half, (k + 1) * half)
            for j in range(nt):
                g = _dot(hns[k], w_in_ref[j])
                u = _dot(hns[k], w_in_ref[nt + j])
                p_scr[rows, j * MXU_N:(j + 1) * MXU_N] = (g * _sigmoid(g) * u).astype(BF16)
            f = _dot_tiles(p_scr[rows, :], w_out_ref, 0, D_MODEL // MXU_N)
            ys_.append(xs_[k] + _rms(f, g_ref[3:4, :]))
        y = jnp.concatenate(ys_, axis=0)

        @pl.when(is_prompt)
        def _():
            yp_ref[...] = y

        @pl.when(jnp.logical_not(is_prompt))
        def _():
            ys_ref[...] = y


def _ffn_layer(xp, xs, gains, w_in, w_out, li, *, tm):
    assert xp.shape[0] % tm == 0 and xs.shape[0] % tm == 0
    n_cast = _n_cast(w_in, w_out)
    n_p, n_s = xp.shape[0] // tm, xs.shape[0] // tm
    p_tile = lambda i: (_tile_index(i, n_cast, n_p), 0)
    s_tile = lambda i: (_tile_index(i, n_cast + n_p, n_s), 0)
    return pl.pallas_call(
        functools.partial(_ffn_kernel, n_cast=n_cast, n_p=n_p),
        grid=(n_cast + n_p + n_s,),
        in_specs=[pl.BlockSpec((tm, D_MODEL), p_tile), pl.BlockSpec((tm, D_MODEL), s_tile),
                  pl.BlockSpec((None,) + gains.shape[1:], lambda i: (li, 0, 0),
                               pipeline_mode=pl.Buffered(1))] + _cast_specs(w_in, w_out, li),
        out_specs=[pl.BlockSpec((tm, D_MODEL), p_tile), pl.BlockSpec((tm, D_MODEL), s_tile)],
        out_shape=[jax.ShapeDtypeStruct(xp.shape, F32), jax.ShapeDtypeStruct(xs.shape, F32)],
        scratch_shapes=_cast_scratch(w_in, w_out) + [pltpu.VMEM((tm, w_out.shape[-2]), BF16)],
        compiler_params=_params(),
        name="swiglu",
    )(xp, xs, gains, w_in, w_out)


_HIST = CONV_W - 1
_NSLAB = D_RNN // LANES


def _seq_pitch(tt):
    return tt if (tt // SUBLANES) % 2 == 1 else tt + SUBLANES


def _rglru_kernel(x_ref, *refs, n_cast, nb, tt, n_sub, fresh):
    if not fresh:
        hist_ref, h0_ref, *refs = refs
    (g_ref, w_in_f32, cw_ref, cb_ref, wax_ref, bax_ref, lam_ref, w_out_f32,
     y_ref, hist_out_ref, h_out_ref,
     w_in_ref, w_out_ref, slab_in, slab_out, xt_scr, gate_scr, a_scr, b_scr, h_scr) = refs
    i = pl.program_id(0)
    m_rows = nb * tt
    pitch = _seq_pitch(tt)
    nh = _HIST * nb
    nt = D_RNN // MXU_N

    @pl.when(i == 0)
    def _():
        xt_scr[0:nh, :] = jnp.zeros((nh, D_RNN), F32) if fresh else hist_ref[...]
        h_scr[...] = jnp.zeros((nb, D_RNN), F32) if fresh else h0_ref[...]

    @pl.when(i < n_cast)
    def _():
        _cast_step(i, w_in_f32, w_out_f32, w_in_ref, w_out_ref)

    def front(k):
        rows = slice(k * m_rows, (k + 1) * m_rows)
        x = x_ref[:, k * tt:(k + 1) * tt, :].reshape(m_rows, D_MODEL)
        hn = _rms(x, g_ref[0:1, :]).astype(BF16)
        gate_scr[rows, :] = _gelu(_dot_tiles(hn, w_in_ref, 0, nt))
        xb = _dot_tiles(hn, w_in_ref, nt, nt)

        for s in range(_NSLAB):
            for b in range(nb):
                slab_in[k, s, b * pitch:b * pitch + tt, :] = (
                    xb[b * tt:(b + 1) * tt, s * LANES:(s + 1) * LANES])
        r0 = nh + k * m_rows
        for t in range(tt):
            for s in range(_NSLAB):
                xt_scr[r0 + t * nb:r0 + (t + 1) * nb, s * LANES:(s + 1) * LANES] = (
                    slab_in[k, s, pl.ds(t, nb, stride=pitch), :])

        xc = cb_ref[...]
        for kk in range(CONV_W):
            c0 = k * m_rows + kk * nb
            xc = xc + cw_ref[kk:kk + 1, :] * xt_scr[c0:c0 + m_rows, :]
        xcb = xc.astype(BF16)

        lam = lam_ref[...]
        c_sp = LRU_C * (jnp.maximum(-lam, 0.0) + jnp.log1p(jnp.exp(-jnp.abs(lam))))
        for h in range(H_B):
            hs = slice(h * BW_B, (h + 1) * BW_B)
            ri = _dot(xcb[:, hs], wax_ref[h]) + bax_ref[h:h + 1, :]
            r = _sigmoid(ri[:, :BW_B])
            ig = _sigmoid(ri[:, BW_B:])
            nla = c_sp[:, hs] * r
            a = jnp.exp2(nla * (-_LOG2E))
            mult = jnp.sqrt(jnp.tanh(nla) * (1.0 + a * a))
            a_scr[rows, hs] = a
            b_scr[rows, hs] = mult * (ig * xc[:, hs])

    def back(k):
        r0 = k * m_rows
        for s in range(_NSLAB):
            ls = slice(s * LANES, (s + 1) * LANES)
            hcur = h_scr[:, ls]
            for t in range(tt):
                tr = slice(r0 + t * nb, r0 + (t + 1) * nb)
                hcur = a_scr[tr, ls] * hcur + b_scr[tr, ls]
                slab_out[k, s, pl.ds(t, nb, stride=pitch), :] = hcur
            h_scr[:, ls] = hcur
        hs_all = jnp.concatenate(
            [jnp.concatenate([slab_out[k, s, b * pitch:b * pitch + tt, :] for b in range(nb)], axis=0)
             for s in range(_NSLAB)], axis=1)
        og = (hs_all * gate_scr[r0:r0 + m_rows, :]).astype(BF16)
        m = _dot_tiles(og, w_out_ref, 0, D_MODEL // MXU_N)
        x = x_ref[:, k * tt:(k + 1) * tt, :].reshape(m_rows, D_MODEL)
        y_ref[:, k * tt:(k + 1) * tt, :] = (x + _rms(m, g_ref[1:2, :])).reshape(nb, tt, D_MODEL)

    @pl.when(i >= n_cast)
    def _():
        for k in range(n_sub):
            front(k)
        new_hist = xt_scr[n_sub * m_rows:n_sub * m_rows + nh, :]
        hist_out_ref[...] = new_hist
        xt_scr[0:nh, :] = new_hist
        for k in range(n_sub):
            back(k)
        h_out_ref[...] = h_scr[...]


def _rglru_layer(x, hist_tm, h0, gains, w_in, cw, cb, wax, bax, lam, w_out, *, tt, n_sub):
    nb, t_len, _ = x.shape
    t_step = tt * n_sub
    assert t_len % t_step == 0 and tt % SUBLANES == 0 and nb % SUBLANES == 0
    pitch = _seq_pitch(tt)
    m_rows = nb * t_step
    nh = _HIST * nb
    n_cast = _n_cast(w_in, w_out)
    n_tiles = t_len // t_step
    time_tile = lambda i: (0, _tile_index(i, n_cast, n_tiles), 0)
    w_in_spec, w_out_spec = _cast_specs(w_in, w_out)
    fresh = hist_tm is None
    assert fresh == (h0 is None)
    state = () if fresh else (hist_tm, h0)
    return pl.pallas_call(
        functools.partial(_rglru_kernel, n_cast=n_cast, nb=nb, tt=tt, n_sub=n_sub, fresh=fresh),
        grid=(n_cast + n_tiles,),
        in_specs=[pl.BlockSpec((nb, t_step, D_MODEL), time_tile)]
        + [_const_spec(s.shape) for s in state] + [
            _const_spec(gains.shape),
            w_in_spec, _const_spec(cw.shape), _const_spec(cb.shape),
            _const_spec(wax.shape), _const_spec(bax.shape), _const_spec(lam.shape),
            w_out_spec,
        ],
        out_specs=[
            pl.BlockSpec((nb, t_step, D_MODEL), time_tile),
            pl.BlockSpec((nh, D_RNN), lambda i: (0, 0)),
            pl.BlockSpec((nb, D_RNN), lambda i: (0, 0)),
        ],
        out_shape=[
            jax.ShapeDtypeStruct((nb, t_len, D_MODEL), F32),
            jax.ShapeDtypeStruct((nh, D_RNN), F32),
            jax.ShapeDtypeStruct((nb, D_RNN), F32),
        ],
        scratch_shapes=_cast_scratch(w_in, w_out) + [
            pltpu.VMEM((n_sub, _NSLAB, nb * pitch, LANES), F32),
            pltpu.VMEM((n_sub, _NSLAB, nb * pitch, LANES), F32),
            pltpu.VMEM((nh + m_rows, D_RNN), F32),
            pltpu.VMEM((m_rows, D_RNN), F32),
            pltpu.VMEM((m_rows, D_RNN), F32),
            pltpu.VMEM((m_rows, D_RNN), F32),
            pltpu.VMEM((nb, D_RNN), F32),
        ],
        compiler_params=_params(),
        name="rglru_mixer",
    )(x, *state, gains, w_in, cw, cb, wax, bax, lam, w_out)


def kernel(x_prompt, x_sample, state_conv, state_h, norm_gains, gm_w_in, gm_b_in, gm_ln_g, gm_ln_b,
           gm_w_s, gm_b_s, gm_w_out, rg_w_in, rg_conv_w, rg_conv_b, rg_w_a, rg_b_a, rg_w_x, rg_b_x,
           rg_lambda, rg_w_out, ffn_w_in, ffn_w_out):
    n_p, t_p, _ = x_prompt.shape
    n_s, t_s, _ = x_sample.shape
    assert CHUNK % t_s == 0 and t_p % CHUNK == 0

    gm_b_in2 = gm_b_in[0].reshape(1, -1)
    gm_ln_g2 = gm_ln_g[0].reshape(1, -1)
    gm_ln_b2 = gm_ln_b[0].reshape(1, -1)
    rep = CHUNK // t_s
    ws_s = jnp.tile(gm_w_s[0][:, :t_s, :t_s], (1, 1, rep))
    bs_t = gm_b_s[0].T
    rg_wax = jnp.concatenate([rg_w_a[0], rg_w_x[0]], axis=-1).astype(BF16)
    rg_bax = jnp.concatenate([rg_b_a[0], rg_b_x[0]], axis=-1)
    rg_cb = rg_conv_b[0].reshape(1, -1)
    rg_lam = rg_lambda[0].reshape(1, -1)

    def ffn(xp, xs, li):
        return _ffn_layer(xp, xs, norm_gains, ffn_w_in, ffn_w_out, li, tm=512)

    def rglru(x, hist, h0, tt, n_sub):
        nb = x.shape[0]
        hist_tm = None if hist is None else jnp.swapaxes(hist, 0, 1).reshape(_HIST * nb, D_RNN)
        y, new_hist_tm, h_last = _rglru_layer(x, hist_tm, h0, norm_gains[1], rg_w_in[0], rg_conv_w[0],
                                              rg_cb, rg_wax, rg_bax, rg_lam, rg_w_out[0],
                                              tt=tt, n_sub=n_sub)
        return y, jnp.swapaxes(new_hist_tm.reshape(_HIST, nb, D_RNN), 0, 1), h_last

    xp = x_prompt.reshape(n_p * t_p, D_MODEL)
    xs = x_sample.reshape(n_s * t_s, D_MODEL)
    xp, xs, s_v = _gmlp_layer(xp, xs, norm_gains[0], gm_w_in[0], gm_b_in2, gm_ln_g2, gm_ln_b2,
                              gm_w_s[0], ws_s, bs_t, gm_w_out[0], tm=256, period_s=t_s)
    xp, xs = ffn(xp, xs, 0)
    xp, p_conv, p_h = rglru(xp.reshape(n_p, t_p, D_MODEL), None, None, 32, 2)
    xs, s_conv, s_h = rglru(xs.reshape(n_s, t_s, D_MODEL), state_conv[0], state_h[0], t_s, 1)
    y_prompt, y_sample = ffn(xp.reshape(n_p * t_p, D_MODEL), xs.reshape(n_s * t_s, D_MODEL), 1)

    return (y_prompt.reshape(n_p, t_p, D_MODEL), y_sample.reshape(n_s, t_s, D_MODEL),
            p_conv[None], p_h[None], s_conv[None], s_h[None], s_v.reshape(1, n_s, t_s, D_GMLP))
```
